```python
import math
import jax
import jax.numpy as jnp
from jax import lax
import numpy as np

D_MODEL = 1024
BATCH = 8
SEQ = 8192
DEPTH = 2

CHUNK = 64
EPS = 1e-6
D_MIX = D_MODEL
HEAD_DIM = 64

A_WIDTH = 3 * D_MIX // 8
A_HEADS = A_WIDTH // HEAD_DIM
A_BLOCK = 128

B_WIDTH = 3 * D_MIX // 8
B_HEADS = B_WIDTH // HEAD_DIM
CONV_WIDTH = 4
LRU_C = 8.0

C_WIDTH = D_MIX - A_WIDTH - B_WIDTH
C_GROUP_DIM = 16
C_GROUPS = C_WIDTH // C_GROUP_DIM
C_STATE = 64

OFF_A_U = 0
OFF_A_V = OFF_A_U + A_WIDTH
OFF_B_X = OFF_A_V + A_WIDTH
OFF_B_G = OFF_B_X + B_WIDTH
OFF_C_X = OFF_B_G + B_WIDTH
IN_COLS = OFF_C_X + C_WIDTH

N_GROUPS = 4
EXPERTS_PER_GROUP = 4
N_EXPERTS = N_GROUPS * EXPERTS_PER_GROUP
TOP_K = 2
D_EXPERT = D_MODEL // 2
MOE_BLOCK = 256

kernel_name = "hybrid_gmlp_rglru_s5_hmoe"


def rms_norm(x, g):
    xf = x.astype(jnp.float32)
    y = xf * lax.rsqrt(jnp.mean(xf * xf, axis=-1, keepdims=True) + EPS)
    return (y * g.astype(jnp.float32)).astype(x.dtype)


def _linear_combine(left, right):
    a_l, b_l = left
    a_r, b_r = right
    return a_r * a_l, a_r * b_l + b_r


def linear_recurrence(a, b):
    _, h = lax.associative_scan(_linear_combine, (a, b), axis=1)
    return h


def causal_depthwise_conv(x, w, b):
    k_w = w.shape[0]
    seq = x.shape[1]
    xp = jnp.pad(x, ((0, 0), (k_w - 1, 0), (0, 0)))
    y = b + w[k_w - 1] * x
    for k in range(k_w - 1):
        y = y + w[k] * xp[:, k:k + seq]
    return y


def mixer_spatial_gating(u_raw, v_raw, v_norm_g, w_s, b_s):
    bsz, seq, _ = u_raw.shape
    u = jax.nn.gelu(u_raw)
    v = rms_norm(jax.nn.gelu(v_raw), v_norm_g)
    v = v.reshape(bsz, seq // A_BLOCK, A_BLOCK, A_HEADS, HEAD_DIM)
    chunk_id = jnp.arange(A_BLOCK) // CHUNK
    mask = chunk_id[None, :] <= chunk_id[:, None]
    w = jnp.where(mask[None], w_s, 0.0)
    sv = jnp.einsum("hij,bnjhd->bnihd", w, v) + b_s.T[:, :, None]
    return u * sv.reshape(bsz, seq, A_WIDTH)


def mixer_rglru(x_raw, g_raw, conv_w, conv_b, rg_w, rg_b, ig_w, ig_b, lam):
    bsz, seq, width = x_raw.shape
    xc = causal_depthwise_conv(x_raw, conv_w, conv_b)
    xh = xc.reshape(bsz, seq, B_HEADS, HEAD_DIM)
    r = jax.nn.sigmoid(jnp.einsum("bshi,hij->bshj", xh, rg_w).reshape(bsz, seq, width) + rg_b)
    i = jax.nn.sigmoid(jnp.einsum("bshi,hij->bshj", xh, ig_w).reshape(bsz, seq, width) + ig_b)
    log_a = -LRU_C * r.astype(jnp.float32) * jax.nn.softplus(-lam.astype(jnp.float32))
    a = jnp.exp(log_a)
    b = jnp.sqrt(-jnp.expm1(2.0 * log_a)) * (i * xc).astype(jnp.float32)
    h = linear_recurrence(a, b)
    return (h * jax.nn.gelu(g_raw.astype(jnp.float32))).astype(x_raw.dtype)


def mixer_s5(x_raw, a_re, a_im, log_dt, b_re, b_im, c_re, c_im, d, glu_w, glu_b):
    bsz, seq, width = x_raw.shape
    f32 = jnp.float32
    xf = x_raw.astype(f32)
    lam = lax.complex(a_re.astype(f32), a_im.astype(f32))
    dt = jnp.exp(log_dt.astype(f32))[:, None]
    lam_bar = jnp.exp(lam * dt)
    b_bar = ((lam_bar - 1.0) / lam)[:, :, None] * lax.complex(b_re.astype(f32), b_im.astype(f32))
    c_mat = lax.complex(c_re.astype(f32), c_im.astype(f32))
    xg = xf.reshape(bsz, seq, C_GROUPS, C_GROUP_DIM)
    bu = jnp.einsum("gpc,bsgc->bsgp", b_bar, xg.astype(jnp.complex64))
    state = linear_recurrence(jnp.broadcast_to(lam_bar, bu.shape), bu)
    y = jnp.einsum("gcp,bsgp->bsgc", c_mat, state).real.reshape(bsz, seq, width) + d.astype(f32) * xf
    y = jax.nn.gelu(y)
    y = y * jax.nn.sigmoid(y @ glu_w.astype(f32) + glu_b.astype(f32))
    return y.astype(x_raw.dtype)


def hybrid_mixer(h, norm_g, w_in, a_v_norm_g, a_spatial_w, a_spatial_b,
                 b_conv_w, b_conv_b, b_rg_w, b_rg_b, b_ig_w, b_ig_b, b_lambda,
                 c_a_re, c_a_im, c_log_dt, c_b_re, c_b_im, c_c_re, c_c_im, c_d, c_glu_w, c_glu_b,
                 mix_out_norm_g, w_out):
    z = rms_norm(h, norm_g) @ w_in
    y_a = mixer_spatial_gating(z[..., OFF_A_U:OFF_A_V], z[..., OFF_A_V:OFF_B_X],
                               a_v_norm_g, a_spatial_w, a_spatial_b)
    y_b = mixer_rglru(z[..., OFF_B_X:OFF_B_G], z[..., OFF_B_G:OFF_C_X],
                      b_conv_w, b_conv_b, b_rg_w, b_rg_b, b_ig_w, b_ig_b, b_lambda)
    y_c = mixer_s5(z[..., OFF_C_X:IN_COLS], c_a_re, c_a_im, c_log_dt, c_b_re, c_b_im,
                   c_c_re, c_c_im, c_d, c_glu_w, c_glu_b)
    y = jnp.concatenate([
        rms_norm(y_a, mix_out_norm_g[:A_WIDTH]),
        rms_norm(y_b, mix_out_norm_g[A_WIDTH:A_WIDTH + B_WIDTH]),
        rms_norm(y_c, mix_out_norm_g[A_WIDTH + B_WIDTH:]),
    ], axis=-1)
    return y @ w_out


def hierarchical_moe(hn, rg_w, rg_b, re_w, re_b, w_gate, w_up, w_down):
    bsz, seq, dm = hn.shape
    n_tok = bsz * seq
    xt = hn.reshape(n_tok, dm)
    f32 = jnp.float32
    g_logits = (xt @ rg_w).astype(f32) + rg_b.astype(f32)
    g_prob = jax.nn.softmax(g_logits, axis=-1)
    g_sel = jnp.argmax(g_logits, axis=-1).astype(jnp.int32)
    g_w = jnp.take_along_axis(g_prob, g_sel[:, None], axis=-1)[:, 0]
    e_all = jnp.einsum("td,gde->tge", xt, re_w).astype(f32) + re_b.astype(f32)
    e_logits = jnp.take_along_axis(e_all, g_sel[:, None, None], axis=1)[:, 0]
    e_top, e_idx = lax.top_k(e_logits, TOP_K)
    e_w = jax.nn.softmax(e_top, axis=-1)
    expert_id = g_sel[:, None] * EXPERTS_PER_GROUP + e_idx.astype(jnp.int32)
    gate = g_w[:, None] * e_w
    n_assign = n_tok * TOP_K
    flat_e = expert_id.reshape(n_assign)
    flat_tok = jnp.arange(n_assign, dtype=jnp.int32) // TOP_K
    flat_gate = gate.reshape(n_assign)
    order = jnp.argsort(flat_e)
    sorted_e = flat_e[order]
    counts = jnp.bincount(flat_e, length=N_EXPERTS)
    start = jnp.cumsum(counts) - counts
    padded = ((counts + MOE_BLOCK - 1) // MOE_BLOCK) * MOE_BLOCK
    pad_end = jnp.cumsum(padded)
    pad_start = pad_end - padded
    dest = pad_start[sorted_e] + (jnp.arange(n_assign, dtype=jnp.int32) - start[sorted_e])
    n_blocks = -(-n_assign // MOE_BLOCK) + N_EXPERTS
    n_rows = n_blocks * MOE_BLOCK
    row_tok = jnp.zeros((n_rows,), jnp.int32).at[dest].set(flat_tok[order])
    row_gate = jnp.zeros((n_rows,), f32).at[dest].set(flat_gate[order])
    block_e = jnp.minimum(
        jnp.searchsorted(pad_end, jnp.arange(n_blocks, dtype=pad_end.dtype) * MOE_BLOCK, side="right"),
        N_EXPERTS - 1).astype(jnp.int32)
    xs = jnp.take(xt, row_tok, axis=0).reshape(n_blocks, MOE_BLOCK, dm)

    def expert_block(args):
        xb, e = args
        return (jax.nn.silu(xb @ w_gate[e]) * (xb @ w_up[e])) @ w_down[e]

    ys = lax.map(expert_block, (xs, block_e)).reshape(n_rows, dm)
    ys = ys * row_gate[:, None].astype(ys.dtype)
    out = jnp.zeros((n_tok, dm), ys.dtype).at[row_tok].add(ys)
    return out.reshape(bsz, seq, dm)


def setup_inputs(seed: int = 0) -> dict:
    key = jax.random.key(seed)
    ks = iter(jax.random.split(key, 48))
    f32 = jnp.float32
    L = DEPTH

    def nrm(shape, scale):
        return scale * jax.random.normal(next(ks), shape, f32)

    x = jax.random.normal(next(ks), (BATCH, SEQ, D_MODEL), f32)
    norm_mix_g = 1.0 + nrm((L, D_MODEL), 0.02)
    w_in = nrm((L, D_MODEL, IN_COLS), D_MODEL ** -0.5)
    a_v_norm_g = 1.0 + nrm((L, A_WIDTH), 0.02)
    a_spatial_w = nrm((L, A_HEADS, A_BLOCK, A_BLOCK), A_BLOCK ** -0.5)
    a_spatial_b = 1.0 + nrm((L, A_HEADS, A_BLOCK), 0.02)
    b_conv_w = nrm((L, CONV_WIDTH, B_WIDTH), CONV_WIDTH ** -0.5)
    b_conv_b = nrm((L, B_WIDTH), 0.02)
    b_rg_w = nrm((L, B_HEADS, HEAD_DIM, HEAD_DIM), HEAD_DIM ** -0.5)
    b_rg_b = nrm((L, B_WIDTH), 0.02)
    b_ig_w = nrm((L, B_HEADS, HEAD_DIM, HEAD_DIM), HEAD_DIM ** -0.5)
    b_ig_b = nrm((L, B_WIDTH), 0.02)
    a_pow = jax.random.uniform(next(ks), (L, B_WIDTH), f32, 0.9, 0.999)
    a0 = a_pow ** (1.0 / LRU_C)
    b_lambda = jnp.log(a0) - jnp.log1p(-a0)
    n_idx = jnp.arange(C_STATE, dtype=f32)
    c_a_re = -0.5 * jnp.exp(nrm((L, C_GROUPS, C_STATE), 0.05))
    c_a_im = math.pi * n_idx + nrm((L, C_GROUPS, C_STATE), 0.01)
    c_log_dt = jax.random.uniform(next(ks), (L, C_GROUPS), f32, math.log(1e-3), math.log(1e-1))
    c_b_re = nrm((L, C_GROUPS, C_STATE, C_GROUP_DIM), (2 * C_GROUP_DIM) ** -0.5)
    c_b_im = nrm((L, C_GROUPS, C_STATE, C_GROUP_DIM), (2 * C_GROUP_DIM) ** -0.5)
    c_c_re = nrm((L, C_GROUPS, C_GROUP_DIM, C_STATE), 0.5)
    c_c_im = nrm((L, C_GROUPS, C_GROUP_DIM, C_STATE), 0.5)
    c_d = nrm((L, C_WIDTH), 1.0)
    c_glu_w = nrm((L, C_WIDTH, C_WIDTH), C_WIDTH ** -0.5)
    c_glu_b = nrm((L, C_WIDTH), 0.02)
    mix_out_norm_g = 1.0 + nrm((L, D_MIX), 0.02)
    w_out = nrm((L, D_MIX, D_MODEL), D_MIX ** -0.5)
    norm_ffn_g = 1.0 + nrm((L, D_MODEL), 0.02)
    router_group_w = nrm((L, D_MODEL, N_GROUPS), D_MODEL ** -0.5)
    router_group_b = nrm((L, N_GROUPS), 0.01)
    router_expert_w = nrm((L, N_GROUPS, D_MODEL, EXPERTS_PER_GROUP), D_MODEL ** -0.5)
    router_expert_b = nrm((L, N_GROUPS, EXPERTS_PER_GROUP), 0.01)
    expert_w_gate = nrm((L, N_EXPERTS, D_MODEL, D_EXPERT), D_MODEL ** -0.5)
    expert_w_up = nrm((L, N_EXPERTS, D_MODEL, D_EXPERT), D_MODEL ** -0.5)
    expert_w_down = nrm((L, N_EXPERTS, D_EXPERT, D_MODEL), D_EXPERT ** -0.5)
    final_norm_g = 1.0 + nrm((D_MODEL,), 0.02)
    return {
        "x": x, "norm_mix_g": norm_mix_g, "w_in": w_in,
        "a_v_norm_g": a_v_norm_g, "a_spatial_w": a_spatial_w, "a_spatial_b": a_spatial_b,
        "b_conv_w": b_conv_w, "b_conv_b": b_conv_b, "b_rg_w": b_rg_w, "b_rg_b": b_rg_b,
        "b_ig_w": b_ig_w, "b_ig_b": b_ig_b, "b_lambda": b_lambda,
        "c_a_re": c_a_re, "c_a_im": c_a_im, "c_log_dt": c_log_dt, "c_b_re": c_b_re, "c_b_im": c_b_im,
        "c_c_re": c_c_re, "c_c_im": c_c_im, "c_d": c_d, "c_glu_w": c_glu_w, "c_glu_b": c_glu_b,
        "mix_out_norm_g": mix_out_norm_g, "w_out": w_out, "norm_ffn_g": norm_ffn_g,
        "router_group_w": router_group_w, "router_group_b": router_group_b,
        "router_expert_w": router_expert_w, "router_expert_b": router_expert_b,
        "expert_w_gate": expert_w_gate, "expert_w_up": expert_w_up, "expert_w_down": expert_w_down,
        "final_norm_g": final_norm_g,
    }


def reference(x, norm_mix_g, w_in, a_v_norm_g, a_spatial_w, a_spatial_b,
              b_conv_w, b_conv_b, b_rg_w, b_rg_b, b_ig_w, b_ig_b, b_lambda,
              c_a_re, c_a_im, c_log_dt, c_b_re, c_b_im, c_c_re, c_c_im, c_d, c_glu_w, c_glu_b,
              mix_out_norm_g, w_out, norm_ffn_g,
              router_group_w, router_group_b, router_expert_w, router_expert_b,
              expert_w_gate, expert_w_up, expert_w_down, final_norm_g):
    h = x
    for l in range(DEPTH):
        mix = hybrid_mixer(h, norm_mix_g[l], w_in[l], a_v_norm_g[l], a_spatial_w[l], a_spatial_b[l],
                           b_conv_w[l], b_conv_b[l], b_rg_w[l], b_rg_b[l], b_ig_w[l], b_ig_b[l], b_lambda[l],
                           c_a_re[l], c_a_im[l], c_log_dt[l], c_b_re[l], c_b_im[l], c_c_re[l], c_c_im[l],
                           c_d[l], c_glu_w[l], c_glu_b[l], mix_out_norm_g[l], w_out[l])
        h = h + mix.astype(h.dtype)
        ffn = hierarchical_moe(rms_norm(h, norm_ffn_g[l]), router_group_w[l], router_group_b[l],
                               router_expert_w[l], router_expert_b[l],
                               expert_w_gate[l], expert_w_up[l], expert_w_down[l])
        h = h + ffn.astype(h.dtype)
    return rms_norm(h, final_norm_g).astype(x.dtype)
```

```python
import functools
import math

import jax
import jax.numpy as jnp
from jax import lax
from jax.experimental import pallas as pl
from jax.experimental.pallas import tpu as pltpu

EPS = 1e-6
CHUNK = 64
A_BLOCK = 128
HEAD_DIM = 64
LRU_C = 8.0
TOKENS_PER_ROW = 8
LANES = 128
N_PAIRS = 6
MIX_TILE = 512
MOE_BLOCK = 256
ROW_TILE = 256
RANK_TILE = 1024
ROUTER_ROWS = 32
VMEM_LIMIT = 56 * 1024 * 1024

f32 = jnp.float32
bf16 = jnp.bfloat16


def _rms(x, g):
    ms = jnp.mean(x * x, axis=-1, keepdims=True)
    return x * lax.rsqrt(ms + EPS) * g


def _dot(a, b):
    return jnp.dot(a, b, preferred_element_type=f32)


def _dot_nt(a, b):
    return lax.dot_general(a, b, (((1,), (1,)), ((), ())), preferred_element_type=f32)


def _mix_kernel(h_ref, ng_ref, win_ref, avg_ref, aw_ref, ab_ref, cw_ref, cb_ref, wg_ref, bg_ref,
                nl_ref, s5a_ref, s5c_ref, pkr_ref, pki_ref, pjr_ref, pji_ref, cd_ref, gw_ref,
                gb_ref, og_ref, wout_ref, fg_ref, wr_ref, wrhi_ref, rb_ref,
                h1_ref, xs_ref, rt_ref,
                xb, gbuf, ybuf, zc, yc, lru_c, s5r_c, s5i_c):
    tt = h_ref.shape[0]
    n = tt // TOKENS_PER_ROW
    n_steps = n.bit_length() - 1
    aw = A_BLOCK * 3
    off_av, off_bx, off_bg, off_cx = aw, 2 * aw, 3 * aw, 4 * aw

    @pl.when(pl.program_id(1) == 0)
    def _():
        xb[:, 0:8, :] = jnp.zeros((3, 8, LANES), f32)
        lru_c[...] = jnp.zeros_like(lru_c)
        s5r_c[...] = jnp.zeros_like(s5r_c)
        s5i_c[...] = jnp.zeros_like(s5i_c)

    h = h_ref[...]
    xn = _rms(h, ng_ref[...]).astype(bf16)
    z = _dot(xn, win_ref[...])

    u = jax.nn.gelu(z[:, 0:aw])
    v = _rms(jax.nn.gelu(z[:, off_av:off_av + aw]), avg_ref[...]).astype(bf16)
    nblk = tt // A_BLOCK
    low_half = (lax.broadcasted_iota(jnp.int32, (A_BLOCK, A_BLOCK * nblk), 1) & (LANES - 1)) < HEAD_DIM
    sv_pairs = []
    for p in range(3):
        rhs = jnp.concatenate(
            [v[b * A_BLOCK:(b + 1) * A_BLOCK, p * LANES:(p + 1) * LANES] for b in range(nblk)], axis=1)
        r0 = _dot(aw_ref[2 * p], rhs)
        r1 = _dot(aw_ref[2 * p + 1], rhs)
        sv_pairs.append(jnp.where(low_half, r0, r1))
    sv = jnp.concatenate(
        [jnp.concatenate([sv_pairs[p][:, b * LANES:(b + 1) * LANES] for p in range(3)], axis=1)
         for b in range(nblk)], axis=0)
    bias_a = jnp.concatenate([ab_ref[...]] * nblk, axis=0)
    na = _rms(u * (sv + bias_a), og_ref[:, 0:aw])

    for j in range(3):
        xb[j, 8:tt + 8, :] = z[:, off_bx + j * LANES:off_bx + (j + 1) * LANES]
        gbuf[j, :, :] = z[:, off_bg + j * LANES:off_bg + (j + 1) * LANES]
    nl = nl_ref[...]
    a_cum, h_loc = [], []
    for s in range(TOKENS_PER_ROW):
        cols = []
        for j in range(3):
            acc = cb_ref[:, j * LANES:(j + 1) * LANES] + cw_ref[3:4, j * LANES:(j + 1) * LANES] * \
                xb[j, pl.ds(8 + s, n, stride=TOKENS_PER_ROW), :]
            for k in range(3):
                acc = acc + cw_ref[k:k + 1, j * LANES:(j + 1) * LANES] * \
                    xb[j, pl.ds(5 + s + k, n, stride=TOKENS_PER_ROW), :]
            cols.append(acc)
        xc = jnp.concatenate(cols, axis=1)
        ri = _dot(xc.astype(bf16), wg_ref[...]) + bg_ref[...]
        r = jax.nn.sigmoid(ri[:, 0:aw])
        ig = jax.nn.sigmoid(ri[:, aw:2 * aw])
        log_a = nl * r
        a = jnp.exp(log_a)
        b = jnp.sqrt(-jnp.tanh(log_a) * (a * a + 1.0)) * (ig * xc)
        if s == 0:
            a_cum.append(a)
            h_loc.append(b)
        else:
            h_loc.append(a * h_loc[-1] + b)
            a_cum.append(a * a_cum[-1])
    row = lax.broadcasted_iota(jnp.int32, (n, 1), 0)
    ac, hc = a_cum[-1], h_loc[-1]
    for k in range(n_steps):
        sh = 1 << k
        keep = row >= sh
        ar = pltpu.roll(ac, sh, 0)
        hr = pltpu.roll(hc, sh, 0)
        hc = jnp.where(keep, ac * hr + hc, hc)
        ac = jnp.where(keep, ac * ar, ac)
    cin = lru_c[0:1, :]
    incl = hc + ac * cin
    c0 = jnp.where(row == 0, cin, pltpu.roll(incl, 1, 0))
    lru_c[0:1, :] = incl[n - 1:n, :]
    for s in range(TOKENS_PER_ROW):
        gate = jnp.concatenate(
            [gbuf[j, pl.ds(s, n, stride=TOKENS_PER_ROW), :] for j in range(3)], axis=1)
        yb = (h_loc[s] + a_cum[s] * c0) * jax.nn.gelu(gate)
        for j in range(3):
            ybuf[j, pl.ds(s, n, stride=TOKENS_PER_ROW), :] = yb[:, j * LANES:(j + 1) * LANES]
    xb[:, 0:8, :] = xb[:, tt:tt + 8, :]
    nb = _rms(jnp.concatenate([ybuf[j] for j in range(3)], axis=1), og_ref[:, aw:2 * aw])

    for hh in range(2):
        zc[hh, :, :] = z[:, off_cx + hh * LANES:off_cx + (hh + 1) * LANES]
    half = 4 * LANES
    e_parts, y_intra = [], []
    for hh in range(2):
        xrow = jnp.concatenate(
            [zc[hh, pl.ds(s, n, stride=TOKENS_PER_ROW), :] for s in range(TOKENS_PER_ROW)],
            axis=1).astype(bf16)
        res = _dot(xrow, s5a_ref[hh])
        e_parts.append(res[:, 0:2 * half])
        y_intra.append(res[:, 2 * half:4 * half])
    er = jnp.concatenate([e_parts[0][:, 0:half], e_parts[1][:, 0:half]], axis=1)
    ei = jnp.concatenate([e_parts[0][:, half:2 * half], e_parts[1][:, half:2 * half]], axis=1)
    for k in range(n_steps):
        sh = 1 << k
        keep = row >= sh
        pr = pkr_ref[k:k + 1, :]
        pi = pki_ref[k:k + 1, :]
        rr = pltpu.roll(er, sh, 0)
        rim = pltpu.roll(ei, sh, 0)
        er, ei = (er + jnp.where(keep, pr * rr - pi * rim, 0.0),
                  ei + jnp.where(keep, pr * rim + pi * rr, 0.0))
    cr = s5r_c[0:1, :]
    ci = s5i_c[0:1, :]
    fr = er + pjr_ref[...] * cr - pji_ref[...] * ci
    fi = ei + pjr_ref[...] * ci + pji_ref[...] * cr
    s0r = jnp.where(row == 0, cr, pltpu.roll(fr, 1, 0))
    s0i = jnp.where(row == 0, ci, pltpu.roll(fi, 1, 0))
    s5r_c[0:1, :] = fr[n - 1:n, :]
    s5i_c[0:1, :] = fi[n - 1:n, :]
    for hh in range(2):
        s0 = jnp.concatenate(
            [s0r[:, hh * half:(hh + 1) * half], s0i[:, hh * half:(hh + 1) * half]],
            axis=1).astype(bf16)
        yrow = y_intra[hh] + _dot(s0, s5c_ref[hh])
        for t in range(TOKENS_PER_ROW):
            yc[hh, pl.ds(t, n, stride=TOKENS_PER_ROW), :] = yrow[:, t * LANES:(t + 1) * LANES]
    ytok = jnp.concatenate([yc[0], yc[1]], axis=1) + cd_ref[...] * z[:, off_cx:off_cx + 2 * LANES]
    yg = jax.nn.gelu(ytok)
    ycm = yg * jax.nn.sigmoid(_dot(yg.astype(bf16), gw_ref[...]) + gb_ref[...])
    nc = _rms(ycm, og_ref[:, 2 * aw:2 * aw + 2 * LANES])

    ycat = jnp.concatenate([na, nb, nc], axis=1).astype(bf16)
    h1 = h + _dot(ycat, wout_ref[...])
    h1_ref[...] = h1
    hn = _rms(h1, fg_ref[...])
    xs_ref[:, 0:hn.shape[1]] = hn

    hn_hi = hn.astype(bf16)
    hn_lo = (hn - hn_hi.astype(f32)).astype(bf16)
    part = _dot_nt(wr_ref[...], hn_hi)
    lt = (part[0:ROUTER_ROWS] + part[ROUTER_ROWS:2 * ROUTER_ROWS]
          + _dot_nt(wrhi_ref[...], hn_lo) + rb_ref[...])
    g = [lt[k:k + 1, :] for k in range(4)]
    gmax = jnp.maximum(jnp.maximum(g[0], g[1]), jnp.maximum(g[2], g[3]))
    sel = jnp.where(g[0] == gmax, 0, jnp.where(g[1] == gmax, 1, jnp.where(g[2] == gmax, 2, 3)))
    g_w = 1.0 / (jnp.exp(g[0] - gmax) + jnp.exp(g[1] - gmax) + jnp.exp(g[2] - gmax) + jnp.exp(g[3] - gmax))
    e = []
    for k in range(4):
        e.append(jnp.where(sel == 0, lt[4 + k:5 + k, :],
                           jnp.where(sel == 1, lt[8 + k:9 + k, :],
                                     jnp.where(sel == 2, lt[12 + k:13 + k, :], lt[16 + k:17 + k, :]))))
    v1 = jnp.maximum(jnp.maximum(e[0], e[1]), jnp.maximum(e[2], e[3]))
    i1 = jnp.where(e[0] == v1, 0, jnp.where(e[1] == v1, 1, jnp.where(e[2] == v1, 2, 3)))
    neg = -jnp.inf
    m = [jnp.where(i1 == k, neg, e[k]) for k in range(4)]
    v2 = jnp.maximum(jnp.maximum(m[0], m[1]), jnp.maximum(m[2], m[3]))
    i2 = jnp.where(m[0] == v2, 0, jnp.where(m[1] == v2, 1, jnp.where(m[2] == v2, 2, 3)))
    t2 = jnp.exp(v2 - v1)
    w1 = 1.0 / (1.0 + t2)
    w2 = t2 / (1.0 + t2)
    first_low = i1 < i2
    lo = jnp.where(first_low, i1, i2)
    hi = jnp.where(first_low, i2, i1)
    gate_a = g_w * jnp.where(first_low, w1, w2)
    gate_b = g_w * jnp.where(first_low, w2, w1)
    pair = jnp.where(lo == 0, 0, jnp.where(lo == 1, 3, 5)) + hi - lo - 1
    bucket = (sel * N_PAIRS + pair).astype(f32)
    srow = lax.broadcasted_iota(jnp.int32, (8, tt), 0)
    rt_ref[...] = jnp.where(srow == 0, bucket, jnp.where(srow == 1, gate_a, jnp.where(srow == 2, gate_b, 0.0)))
    lrow = lax.broadcasted_iota(jnp.int32, (LANES, tt), 0)
    gates_t = jnp.where(lrow == 0, gate_a, jnp.where(lrow == 1, gate_b, 0.0))
    xs_ref[:, hn.shape[1]:hn.shape[1] + LANES] = gates_t.T


def _const_spec(shape):
    nd = len(shape)
    return pl.BlockSpec(shape, lambda b, i: (0,) * nd, pipeline_mode=pl.Buffered(1))


def _mix_call(h, wts, batch, seq):
    t, d = h.shape
    tt = MIX_TILE
    n_seq = seq // tt
    n = tt // TOKENS_PER_ROW
    row_spec = lambda w: pl.BlockSpec((tt, w), lambda b, i: (b * n_seq + i, 0))
    in_specs = [row_spec(d)] + [_const_spec(w.shape) for w in wts]
    out_shape = (jax.ShapeDtypeStruct((t, d), f32),
                 jax.ShapeDtypeStruct((t, d + LANES), f32),
                 jax.ShapeDtypeStruct((8, t), f32))
    out_specs = (row_spec(d), row_spec(d + LANES),
                 pl.BlockSpec((8, tt), lambda b, i: (0, b * n_seq + i)))
    scratch = [pltpu.VMEM((3, tt + 8, LANES), f32), pltpu.VMEM((3, tt, LANES), f32),
               pltpu.VMEM((3, tt, LANES), f32), pltpu.VMEM((2, tt, LANES), f32),
               pltpu.VMEM((2, tt, LANES), f32), pltpu.VMEM((8, 3 * LANES), f32),
               pltpu.VMEM((8, d), f32), pltpu.VMEM((8, d), f32)]
    return pl.pallas_call(
        _mix_kernel, grid=(batch, n_seq), in_specs=in_specs, out_specs=out_specs,
        out_shape=out_shape, scratch_shapes=scratch, name="mix",
        compiler_params=pltpu.CompilerParams(
            dimension_semantics=("arbitrary", "arbitrary"), vmem_limit_bytes=VMEM_LIMIT),
    )(h, *wts)


def _rank_kernel(rt_ref, tri_ref, rank_ref, cnt_ref, carry):
    @pl.when(pl.program_id(0) == 0)
    def _():
        carry[...] = jnp.zeros_like(carry)

    tr = rt_ref.shape[1]
    bk = rt_ref[0:1, :].astype(jnp.int32)
    rows = lax.broadcasted_iota(jnp.int32, (ROUTER_ROWS, tr), 0)
    hit = rows == bk
    oh = jnp.where(hit, 1.0, 0.0)
    before = _dot(oh.astype(bf16), tri_ref[...]) + carry[:, 0:1]
    rank = jnp.sum(jnp.where(hit, before, 0.0), axis=0, keepdims=True)
    rank_ref[...] = jnp.broadcast_to(rank, (8, tr)).astype(jnp.int32)
    carry[...] = carry[...] + jnp.sum(oh, axis=1, keepdims=True)
    cnt_ref[...] = carry[...]


def _rank_call(rt, tri):
    t = rt.shape[1]
    tr = tri.shape[0]
    return pl.pallas_call(
        _rank_kernel, grid=(t // tr,),
        in_specs=[pl.BlockSpec((8, tr), lambda i: (0, i)), pl.BlockSpec((tr, tr), lambda i: (0, 0))],
        out_specs=(pl.BlockSpec((8, tr), lambda i: (0, i)),
                   pl.BlockSpec((ROUTER_ROWS, LANES), lambda i: (0, 0))),
        out_shape=(jax.ShapeDtypeStruct((8, t), jnp.int32),
                   jax.ShapeDtypeStruct((ROUTER_ROWS, LANES), f32)),
        scratch_shapes=[pltpu.VMEM((ROUTER_ROWS, LANES), f32)], name="rank",
        compiler_params=pltpu.CompilerParams(dimension_semantics=("arbitrary",)),
    )(rt, tri)


def _dispatch_kernel(dest_ref, src_ref, xs_in_ref, xs_out_ref, sem):
    del xs_in_ref
    rows = src_ref.shape[0]

    def row_copy(r, d):
        return pltpu.make_async_copy(src_ref.at[pl.ds(r, 1), :], xs_out_ref.at[pl.ds(d, 1), :], sem)

    def start(r, carry):
        row_copy(r, dest_ref[0, 0, r]).start()
        return carry

    def wait(r, carry):
        row_copy(0, 0).wait()
        return carry

    lax.fori_loop(0, rows, start, 0)
    lax.fori_loop(0, rows, wait, 0)


def _dispatch_call(dest3, src, xs_init):
    t, w = src.shape
    td = dest3.shape[2]
    return pl.pallas_call(
        _dispatch_kernel, grid=(t // td,),
        in_specs=[pl.BlockSpec((1, 1, td), lambda i: (i, 0, 0), memory_space=pltpu.SMEM),
                  pl.BlockSpec((td, w), lambda i: (i, 0)),
                  pl.BlockSpec(memory_space=pl.ANY)],
        out_specs=pl.BlockSpec(memory_space=pl.ANY),
        out_shape=jax.ShapeDtypeStruct(xs_init.shape, xs_init.dtype),
        scratch_shapes=[pltpu.SemaphoreType.DMA(())],
        input_output_aliases={2: 0}, name="dispatch",
        compiler_params=pltpu.CompilerParams(dimension_semantics=("arbitrary",)),
    )(dest3, src, xs_init)


def _expert_kernel(bidx_ref, ea_ref, eb_ref, nv_ref, xs_ref, wga_ref, wua_ref, wda_ref,
                   wgb_ref, wub_ref, wdb_ref, ys_ref):
    del bidx_ref, ea_ref, eb_ref

    @pl.when(pl.program_id(0) < nv_ref[0])
    def _():
        d = ys_ref.shape[1]
        x = xs_ref[:, 0:d].astype(bf16)
        gate_a = xs_ref[:, d:d + 1]
        gate_b = xs_ref[:, d + 1:d + 2]

        def mlp(wg, wu, wd):
            act = jax.nn.silu(_dot(x, wg[...])) * _dot(x, wu[...])
            return _dot(act.astype(bf16), wd[...])

        ys_ref[...] = mlp(wga_ref, wua_ref, wda_ref) * gate_a + mlp(wgb_ref, wub_ref, wdb_ref) * gate_b

    @pl.when(pl.program_id(0) >= nv_ref[0])
    def _():
        ys_ref[...] = jnp.zeros_like(ys_ref)


def _expert_call(bidx, ea, eb, nv, xs, w_gate, w_up, w_down):
    n_rows, w = xs.shape
    d, de = w_gate.shape[1], w_gate.shape[2]
    nb = n_rows // MOE_BLOCK
    wspec = lambda shape, which: pl.BlockSpec(
        (None,) + shape, (lambda b, bi, ea_, eb_, nv_: ((ea_ if which == 0 else eb_)[b], 0, 0)))
    grid_spec = pltpu.PrefetchScalarGridSpec(
        num_scalar_prefetch=4, grid=(nb,),
        in_specs=[pl.BlockSpec((MOE_BLOCK, w), lambda b, bi, ea_, eb_, nv_: (bi[b], 0)),
                  wspec((d, de), 0), wspec((d, de), 0), wspec((de, d), 0),
                  wspec((d, de), 1), wspec((d, de), 1), wspec((de, d), 1)],
        out_specs=pl.BlockSpec((MOE_BLOCK, d), lambda b, bi, ea_, eb_, nv_: (b, 0)))
    return pl.pallas_call(
        _expert_kernel, grid_spec=grid_spec,
        out_shape=jax.ShapeDtypeStruct((n_rows, d), f32), name="experts",
        compiler_params=pltpu.CompilerParams(
            dimension_semantics=("arbitrary",), vmem_limit_bytes=VMEM_LIMIT),
    )(bidx, ea, eb, nv, xs, w_gate, w_up, w_down, w_gate, w_up, w_down)


def _combine_kernel(dest_ref, h1_ref, fg_ref, ys_ref, out_ref, buf, sem, *, final_norm):
    rows = h1_ref.shape[0]

    def row_copy(r, d):
        return pltpu.make_async_copy(ys_ref.at[pl.ds(d, 1), :], buf.at[pl.ds(r, 1), :], sem)

    def start(r, carry):
        row_copy(r, dest_ref[0, 0, r]).start()
        return carry

    def wait(r, carry):
        row_copy(0, 0).wait()
        return carry

    lax.fori_loop(0, rows, start, 0)
    lax.fori_loop(0, rows, wait, 0)
    h2 = h1_ref[...] + buf[...]
    out_ref[...] = _rms(h2, fg_ref[...]) if final_norm else h2


def _combine_call(dest3, h1, fg, ys, final_norm):
    t, d = h1.shape
    tc = dest3.shape[2]
    return pl.pallas_call(
        functools.partial(_combine_kernel, final_norm=final_norm), grid=(t // tc,),
        in_specs=[pl.BlockSpec((1, 1, tc), lambda i: (i, 0, 0), memory_space=pltpu.SMEM),
                  pl.BlockSpec((tc, d), lambda i: (i, 0)),
                  pl.BlockSpec((1, d), lambda i: (0, 0)),
                  pl.BlockSpec(memory_space=pl.ANY)],
        out_specs=pl.BlockSpec((tc, d), lambda i: (i, 0)),
        out_shape=jax.ShapeDtypeStruct((t, d), f32),
        scratch_shapes=[pltpu.VMEM((tc, d), f32), pltpu.SemaphoreType.DMA(())], name="combine",
        compiler_params=pltpu.CompilerParams(dimension_semantics=("arbitrary",)),
    )(dest3, h1, fg, ys)


def _cmul(a, b):
    return a[0] * b[0] - a[1] * b[1], a[0] * b[1] + a[1] * b[0]


def _s5_tables(a_re, a_im, log_dt, b_re, b_im, c_re, c_im, n):
    groups, states = a_re.shape
    chans = b_re.shape[2]
    steps = TOKENS_PER_ROW
    gh = groups // 2
    dt = jnp.exp(log_dt)[:, None]
    ldr, ldi = a_re * dt, a_im * dt

    def lam_pow(k):
        mag = jnp.exp(ldr * k)
        return mag * jnp.cos(ldi * k), mag * jnp.sin(ldi * k)

    lbr, lbi = lam_pow(1.0)
    den = a_re * a_re + a_im * a_im
    qr = ((lbr - 1.0) * a_re + lbi * a_im) / den
    qi = (lbi * a_re - (lbr - 1.0) * a_im) / den
    bbar = _cmul((qr[:, :, None], qi[:, :, None]), (b_re, b_im))
    pw = [lam_pow(float(k)) for k in range(steps + 1)]
    eye = jnp.eye(gh, dtype=f32)

    wb = jnp.stack([jnp.stack(_cmul((pw[steps - 1 - s][0][:, :, None], pw[steps - 1 - s][1][:, :, None]), bbar), 0)
                    for s in range(steps)], 0)
    lagk = []
    for lag in range(steps):
        lb = _cmul((pw[lag][0][:, :, None], pw[lag][1][:, :, None]), bbar)
        lagk.append(jnp.sum(c_re[:, :, :, None] * lb[0][:, None, :, :]
                            - c_im[:, :, :, None] * lb[1][:, None, :, :], axis=2))
    lagk = jnp.stack(lagk, 0)
    s_idx = jnp.arange(steps)[:, None]
    t_idx = jnp.arange(steps)[None, :]
    toe = jnp.where((s_idx <= t_idx)[:, :, None, None, None],
                    lagk[jnp.clip(t_idx - s_idx, 0, steps - 1)], 0.0)
    wc = []
    for t in range(steps):
        cl = _cmul((c_re, c_im), (pw[t + 1][0][:, None, :], pw[t + 1][1][:, None, :]))
        wc.append(jnp.stack([cl[0], -cl[1]], 0))
    wc = jnp.stack(wc, 0)

    s5a, s5c = [], []
    for hh in range(2):
        gs = slice(hh * gh, (hh + 1) * gh)
        wb_h = jnp.einsum("ab,srapc->sacrbp", eye, wb[:, :, gs]).reshape(steps * gh * chans, 2 * gh * states)
        toe_h = jnp.einsum("ab,stacd->sadtbc", eye, toe[:, :, gs]).reshape(steps * gh * chans, steps * gh * chans)
        wc_h = jnp.einsum("ab,tracp->raptbc", eye, wc[:, :, gs]).reshape(2 * gh * states, steps * gh * chans)
        s5a.append(jnp.concatenate([wb_h, toe_h], axis=1))
        s5c.append(wc_h)
    s5a = jnp.stack(s5a, 0).astype(bf16)
    s5c = jnp.stack(s5c, 0).astype(bf16)

    n_steps = n.bit_length() - 1
    flat = lambda x: x.reshape(1, groups * states)
    pk = [lam_pow(float(steps * (1 << k))) for k in range(n_steps)]
    pkr = jnp.concatenate([flat(p[0]) for p in pk], 0)
    pki = jnp.concatenate([flat(p[1]) for p in pk], 0)
    jj = (jnp.arange(n, dtype=f32) + 1.0)[:, None] * steps
    mag = jnp.exp(flat(ldr) * jj)
    pjr = mag * jnp.cos(flat(ldi) * jj)
    pji = mag * jnp.sin(flat(ldi) * jj)
    return s5a, s5c, pkr, pki, pjr, pji


def _layer_weights(l, p, n):
    row = lambda x: x.reshape(1, -1).astype(f32)
    heads = p["a_spatial_w"].shape[1]
    cid = jnp.arange(A_BLOCK) // CHUNK
    aw = jnp.where((cid[None, :] <= cid[:, None])[None], p["a_spatial_w"][l], 0.0).astype(bf16)
    ab = jnp.repeat(p["a_spatial_b"][l].T, HEAD_DIM, axis=1)
    eye_h = jnp.eye(heads, dtype=f32)
    wg = jnp.concatenate([jnp.einsum("ab,aij->aibj", eye_h, p["b_rg_w"][l]).reshape(heads * HEAD_DIM, -1),
                          jnp.einsum("ab,aij->aibj", eye_h, p["b_ig_w"][l]).reshape(heads * HEAD_DIM, -1)],
                         axis=1).astype(bf16)
    bg = jnp.concatenate([p["b_rg_b"][l], p["b_ig_b"][l]]).reshape(1, -1)
    nl = row(-LRU_C * jax.nn.softplus(-p["b_lambda"][l]))
    s5a, s5c, pkr, pki, pjr, pji = _s5_tables(
        p["c_a_re"][l], p["c_a_im"][l], p["c_log_dt"][l], p["c_b_re"][l], p["c_b_im"][l],
        p["c_c_re"][l], p["c_c_im"][l], n)
    d = p["w_in"].shape[1]
    wr = jnp.concatenate([p["router_group_w"][l].T,
                          jnp.transpose(p["router_expert_w"][l], (0, 2, 1)).reshape(-1, d)], 0)
    wr = jnp.concatenate([wr, jnp.zeros((ROUTER_ROWS - wr.shape[0], d), f32)], 0)
    wr_hi = wr.astype(bf16)
    wr_lo = (wr - wr_hi.astype(f32)).astype(bf16)
    rb = jnp.concatenate([p["router_group_b"][l], p["router_expert_b"][l].reshape(-1)])
    rb = jnp.concatenate([rb, jnp.zeros((ROUTER_ROWS - rb.shape[0],), f32)]).reshape(-1, 1)
    return [row(p["norm_mix_g"][l]), p["w_in"][l].astype(bf16), row(p["a_v_norm_g"][l]), aw, ab,
            p["b_conv_w"][l], row(p["b_conv_b"][l]), wg, bg, nl, s5a, s5c, pkr, pki, pjr, pji,
            row(p["c_d"][l]), p["c_glu_w"][l].astype(bf16), row(p["c_glu_b"][l]),
            row(p["mix_out_norm_g"][l]), p["w_out"][l].astype(bf16), row(p["norm_ffn_g"][l]),
            jnp.concatenate([wr_hi, wr_lo], 0), wr_hi, rb]


def _moe(h1, xs_rows, rt, rank_tri, fg, w_gate, w_up, w_down, final_norm):
    t, d = h1.shape
    n_buckets = 4 * N_PAIRS
    rank8, cnt = _rank_call(rt, rank_tri)
    bucket = rt[0].astype(jnp.int32)
    counts = cnt[:n_buckets, 0].astype(jnp.int32)
    padded = ((counts + MOE_BLOCK - 1) // MOE_BLOCK) * MOE_BLOCK
    pad_end = jnp.cumsum(padded)
    pad_start = pad_end - padded
    dest = pad_start[bucket] + rank8[0]
    n_blocks = t // MOE_BLOCK + n_buckets
    nv = (pad_end[-1] // MOE_BLOCK).astype(jnp.int32)
    bidx = jnp.minimum(jnp.arange(n_blocks, dtype=jnp.int32), nv - 1)
    blk_bucket = jnp.minimum(
        jnp.searchsorted(pad_end, bidx * MOE_BLOCK, side="right"), n_buckets - 1).astype(jnp.int32)
    lo_tab = jnp.array([0, 0, 0, 1, 1, 2], jnp.int32)
    hi_tab = jnp.array([1, 2, 3, 2, 3, 3], jnp.int32)
    grp, pr = blk_bucket // N_PAIRS, blk_bucket % N_PAIRS
    ea = grp * 4 + lo_tab[pr]
    eb = grp * 4 + hi_tab[pr]
    dest3 = dest.reshape(t // ROW_TILE, 1, ROW_TILE)
    xs = _dispatch_call(dest3, xs_rows, jnp.zeros((n_blocks * MOE_BLOCK, xs_rows.shape[1]), f32))
    ys = _expert_call(bidx, ea, eb, nv.reshape(1), xs, w_gate, w_up, w_down)
    return _combine_call(dest3, h1, fg, ys, final_norm)


def kernel(x, norm_mix_g, w_in, a_v_norm_g, a_spatial_w, a_spatial_b, b_conv_w, b_conv_b, b_rg_w, b_rg_b, b_ig_w, b_ig_b, b_lambda, c_a_re, c_a_im, c_log_dt, c_b_re, c_b_im, c_c_re, c_c_im, c_d, c_glu_w, c_glu_b, mix_out_norm_g, w_out, norm_ffn_g, router_group_w, router_group_b, router_expert_w, router_expert_b, expert_w_gate, expert_w_up, expert_w_down, final_norm_g):
    p = dict(locals())
    batch, seq, d = x.shape
    depth = w_in.shape[0]
    assert seq % MIX_TILE == 0 and (batch * seq) % RANK_TILE == 0
    n = MIX_TILE // TOKENS_PER_ROW
    idx = jnp.arange(RANK_TILE)
    rank_tri = (idx[:, None] < idx[None, :]).astype(bf16)
    h = x.reshape(batch * seq, d)
    for l in range(depth):
        h1, xs_rows, rt = _mix_call(h, _layer_weights(l, p, n), batch, seq)
        last = l == depth - 1
        fg = (final_norm_g if last else norm_ffn_g[l]).reshape(1, d)
        h = _moe(h1, xs_rows, rt, rank_tri, fg, expert_w_gate[l].astype(bf16),
                 expert_w_up[l].astype(bf16), expert_w_down[l].astype(bf16), last)
    return h.reshape(batch, seq, d)
```

```python
import functools

import jax
import jax.numpy as jnp
from jax import lax
from jax.experimental import pallas as pl
from jax.experimental.pallas import tpu as pltpu

EPS = 1e-6
CHUNK = 64
A_BLOCK = 128
HEAD_DIM = 64
LRU_C = 8.0
TOKENS_PER_ROW = 8
SLAB_ROWS = 8
LANES = 128
N_PAIRS = 6
N_BUCKETS = 4 * N_PAIRS
MIX_TILE = 512
MOE_BLOCK = 256
RANK_TILE = 1024
INV_TILE = 2048
ROUTER_ROWS = 32
VMEM_LIMIT = 56 * 1024 * 1024

f32 = jnp.float32
bf16 = jnp.bfloat16
u32 = jnp.uint32


def _rms(x, g):
    ms = jnp.mean(x * x, axis=-1, keepdims=True)
    return x * lax.rsqrt(ms + EPS) * g


def _dot(a, b):
    return jnp.dot(a, b, preferred_element_type=f32)


def _dot_nt(a, b):
    return lax.dot_general(a, b, (((1,), (1,)), ((), ())), preferred_element_type=f32)


def _slab_rows(ref, rows):
    return jnp.concatenate([ref[pl.ds(s, rows, stride=SLAB_ROWS), :] for s in range(SLAB_ROWS)], axis=1)


def _mix_kernel(*refs, has_res):
    if has_res:
        _mix_body(*refs[1:], res_ref=refs[0])
    else:
        _mix_body(*refs, res_ref=None)


def _mix_body(h_ref, ng_ref, win_ref, avg_ref, aw_ref, ab_ref, cw_ref, cb_ref, wg_ref, bg_ref,
              nl_ref, s5a_ref, s5c_ref, pkr_ref, pki_ref, pjr_ref, pji_ref, cd_ref, gw_ref,
              gb_ref, og_ref, wout_ref, fg_ref, wr_ref, wrhi_ref, rb_ref,
              h1_ref, xs_ref, rt_ref,
              xb, gbuf, ybuf, zc, yc, lru_c, s5r_c, s5i_c, *, res_ref):
    tt = h_ref.shape[0]
    n = tt // TOKENS_PER_ROW
    n_steps = n.bit_length() - 1
    aw = A_BLOCK * 3
    off_av, off_bx, off_bg, off_cx = aw, 2 * aw, 3 * aw, 4 * aw

    @pl.when(pl.program_id(1) == 0)
    def _():
        xb[:, 0:8, :] = jnp.zeros((3, 8, LANES), f32)
        lru_c[...] = jnp.zeros_like(lru_c)
        s5r_c[...] = jnp.zeros_like(s5r_c)
        s5i_c[...] = jnp.zeros_like(s5i_c)

    h = h_ref[...]
    if res_ref is not None:
        h = h + _slab_rows(res_ref, tt)
    xn = _rms(h, ng_ref[...]).astype(bf16)
    z = _dot(xn, win_ref[...])

    u = jax.nn.gelu(z[:, 0:aw])
    v = _rms(jax.nn.gelu(z[:, off_av:off_av + aw]), avg_ref[...]).astype(bf16)
    nblk = tt // A_BLOCK
    low_half = (lax.broadcasted_iota(jnp.int32, (A_BLOCK, A_BLOCK * nblk), 1) & (LANES - 1)) < HEAD_DIM
    sv_pairs = []
    for p in range(3):
        rhs = jnp.concatenate(
            [v[b * A_BLOCK:(b + 1) * A_BLOCK, p * LANES:(p + 1) * LANES] for b in range(nblk)], axis=1)
        r0 = _dot(aw_ref[2 * p], rhs)
        r1 = _dot(aw_ref[2 * p + 1], rhs)
        sv_pairs.append(jnp.where(low_half, r0, r1))
    sv = jnp.concatenate(
        [jnp.concatenate([sv_pairs[p][:, b * LANES:(b + 1) * LANES] for p in range(3)], axis=1)
         for b in range(nblk)], axis=0)
    bias_a = jnp.concatenate([ab_ref[...]] * nblk, axis=0)
    na = _rms(u * (sv + bias_a), og_ref[:, 0:aw])

    for j in range(3):
        xb[j, 8:tt + 8, :] = z[:, off_bx + j * LANES:off_bx + (j + 1) * LANES]
        gbuf[j, :, :] = z[:, off_bg + j * LANES:off_bg + (j + 1) * LANES]
    nl = nl_ref[...]
    a_cum, h_loc = [], []
    for s in range(TOKENS_PER_ROW):
        cols = []
        for j in range(3):
            acc = cb_ref[:, j * LANES:(j + 1) * LANES] + cw_ref[3:4, j * LANES:(j + 1) * LANES] * \
                xb[j, pl.ds(8 + s, n, stride=TOKENS_PER_ROW), :]
            for k in range(3):
                acc = acc + cw_ref[k:k + 1, j * LANES:(j + 1) * LANES] * \
                    xb[j, pl.ds(5 + s + k, n, stride=TOKENS_PER_ROW), :]
            cols.append(acc)
        xc = jnp.concatenate(cols, axis=1)
        ri = _dot(xc.astype(bf16), wg_ref[...]) + bg_ref[...]
        r = jax.nn.sigmoid(ri[:, 0:aw])
        ig = jax.nn.sigmoid(ri[:, aw:2 * aw])
        log_a = nl * r
        a = jnp.exp(log_a)
        b = jnp.sqrt(-jnp.tanh(log_a) * (a * a + 1.0)) * (ig * xc)
        if s == 0:
            a_cum.append(a)
            h_loc.append(b)
        else:
            h_loc.append(a * h_loc[-1] + b)
            a_cum.append(a * a_cum[-1])
    row = lax.broadcasted_iota(jnp.int32, (n, 1), 0)
    ac, hc = a_cum[-1], h_loc[-1]
    for k in range(n_steps):
        sh = 1 << k
        keep = row >= sh
        ar = pltpu.roll(ac, sh, 0)
        hr = pltpu.roll(hc, sh, 0)
        hc = jnp.where(keep, ac * hr + hc, hc)
        ac = jnp.where(keep, ac * ar, ac)
    cin = lru_c[0:1, :]
    incl = hc + ac * cin
    c0 = jnp.where(row == 0, cin, pltpu.roll(incl, 1, 0))
    lru_c[0:1, :] = incl[n - 1:n, :]
    for s in range(TOKENS_PER_ROW):
        gate = jnp.concatenate(
            [gbuf[j, pl.ds(s, n, stride=TOKENS_PER_ROW), :] for j in range(3)], axis=1)
        yb = (h_loc[s] + a_cum[s] * c0) * jax.nn.gelu(gate)
        for j in range(3):
            ybuf[j, pl.ds(s, n, stride=TOKENS_PER_ROW), :] = yb[:, j * LANES:(j + 1) * LANES]
    xb[:, 0:8, :] = xb[:, tt:tt + 8, :]
    nb = _rms(jnp.concatenate([ybuf[j] for j in range(3)], axis=1), og_ref[:, aw:2 * aw])

    for hh in range(2):
        zc[hh, :, :] = z[:, off_cx + hh * LANES:off_cx + (hh + 1) * LANES]
    half = 4 * LANES
    e_parts, y_intra = [], []
    for hh in range(2):
        xrow = jnp.concatenate(
            [zc[hh, pl.ds(s, n, stride=TOKENS_PER_ROW), :] for s in range(TOKENS_PER_ROW)],
            axis=1).astype(bf16)
        res = _dot(xrow, s5a_ref[hh])
        e_parts.append(res[:, 0:2 * half])
        y_intra.append(res[:, 2 * half:4 * half])
    er = jnp.concatenate([e_parts[0][:, 0:half], e_parts[1][:, 0:half]], axis=1)
    ei = jnp.concatenate([e_parts[0][:, half:2 * half], e_parts[1][:, half:2 * half]], axis=1)
    for k in range(n_steps):
        sh = 1 << k
        keep = row >= sh
        pr = pkr_ref[k:k + 1, :]
        pi = pki_ref[k:k + 1, :]
        rr = pltpu.roll(er, sh, 0)
        rim = pltpu.roll(ei, sh, 0)
        er, ei = (er + jnp.where(keep, pr * rr - pi * rim, 0.0),
                  ei + jnp.where(keep, pr * rim + pi * rr, 0.0))
    cr = s5r_c[0:1, :]
    ci = s5i_c[0:1, :]
    fr = er + pjr_ref[...] * cr - pji_ref[...] * ci
    fi = ei + pjr_ref[...] * ci + pji_ref[...] * cr
    s0r = jnp.where(row == 0, cr, pltpu.roll(fr, 1, 0))
    s0i = jnp.where(row == 0, ci, pltpu.roll(fi, 1, 0))
    s5r_c[0:1, :] = fr[n - 1:n, :]
    s5i_c[0:1, :] = fi[n - 1:n, :]
    for hh in range(2):
        s0 = jnp.concatenate(
            [s0r[:, hh * half:(hh + 1) * half], s0i[:, hh * half:(hh + 1) * half]],
            axis=1).astype(bf16)
        yrow = y_intra[hh] + _dot(s0, s5c_ref[hh])
        for t in range(TOKENS_PER_ROW):
            yc[hh, pl.ds(t, n, stride=TOKENS_PER_ROW), :] = yrow[:, t * LANES:(t + 1) * LANES]
    ytok = jnp.concatenate([yc[0], yc[1]], axis=1) + cd_ref[...] * z[:, off_cx:off_cx + 2 * LANES]
    yg = jax.nn.gelu(ytok)
    ycm = yg * jax.nn.sigmoid(_dot(yg.astype(bf16), gw_ref[...]) + gb_ref[...])
    nc = _rms(ycm, og_ref[:, 2 * aw:2 * aw + 2 * LANES])

    ycat = jnp.concatenate([na, nb, nc], axis=1).astype(bf16)
    h1 = h + _dot(ycat, wout_ref[...])
    h1_ref[...] = h1
    hn = _rms(h1, fg_ref[...])

    hn_hi = hn.astype(bf16)
    hn_lo = (hn - hn_hi.astype(f32)).astype(bf16)
    part = _dot_nt(wr_ref[...], hn_hi)
    lt = (part[0:ROUTER_ROWS] + part[ROUTER_ROWS:2 * ROUTER_ROWS]
          + _dot_nt(wrhi_ref[...], hn_lo) + rb_ref[...])
    g = [lt[k:k + 1, :] for k in range(4)]
    gmax = jnp.maximum(jnp.maximum(g[0], g[1]), jnp.maximum(g[2], g[3]))
    sel = jnp.where(g[0] == gmax, 0, jnp.where(g[1] == gmax, 1, jnp.where(g[2] == gmax, 2, 3)))
    g_w = 1.0 / (jnp.exp(g[0] - gmax) + jnp.exp(g[1] - gmax) + jnp.exp(g[2] - gmax) + jnp.exp(g[3] - gmax))
    e = []
    for k in range(4):
        e.append(jnp.where(sel == 0, lt[4 + k:5 + k, :],
                           jnp.where(sel == 1, lt[8 + k:9 + k, :],
                                     jnp.where(sel == 2, lt[12 + k:13 + k, :], lt[16 + k:17 + k, :]))))
    v1 = jnp.maximum(jnp.maximum(e[0], e[1]), jnp.maximum(e[2], e[3]))
    i1 = jnp.where(e[0] == v1, 0, jnp.where(e[1] == v1, 1, jnp.where(e[2] == v1, 2, 3)))
    neg = -jnp.inf
    m = [jnp.where(i1 == k, neg, e[k]) for k in range(4)]
    v2 = jnp.maximum(jnp.maximum(m[0], m[1]), jnp.maximum(m[2], m[3]))
    i2 = jnp.where(m[0] == v2, 0, jnp.where(m[1] == v2, 1, jnp.where(m[2] == v2, 2, 3)))
    t2 = jnp.exp(v2 - v1)
    w1 = 1.0 / (1.0 + t2)
    w2 = t2 / (1.0 + t2)
    first_low = i1 < i2
    lo = jnp.where(first_low, i1, i2)
    hi = jnp.where(first_low, i2, i1)
    gate_a = g_w * jnp.where(first_low, w1, w2)
    gate_b = g_w * jnp.where(first_low, w2, w1)
    pair = jnp.where(lo == 0, 0, jnp.where(lo == 1, 3, 5)) + hi - lo - 1
    bucket = (sel * N_PAIRS + pair).astype(f32)
    srow = lax.broadcasted_iota(jnp.int32, (8, tt), 0)
    rt_ref[...] = jnp.where(srow == 0, bucket, 0.0)
    lrow = lax.broadcasted_iota(jnp.int32, (LANES, tt), 0)
    gates_t = jnp.where(lrow == 0, gate_a, jnp.where(lrow == 1, gate_b, 0.0))

    bits = pltpu.bitcast(hn_hi.astype(f32), u32)
    dh = hn.shape[1] // 2
    words = (bits[:, 0:dh] >> 16) | (bits[:, dh:2 * dh] & u32(0xFFFF0000))
    feat_rows = dh // LANES
    for s in range(feat_rows):
        xs_ref[pl.ds(s, tt, stride=SLAB_ROWS), :] = words[:, s * LANES:(s + 1) * LANES]
    xs_ref[pl.ds(feat_rows, tt, stride=SLAB_ROWS), :] = pltpu.bitcast(gates_t.T, u32)
    for s in range(feat_rows + 1, SLAB_ROWS):
        xs_ref[pl.ds(s, tt, stride=SLAB_ROWS), :] = jnp.zeros((tt, LANES), u32)


def _const_spec(shape):
    nd = len(shape)
    return pl.BlockSpec(shape, lambda b, i: (0,) * nd, pipeline_mode=pl.Buffered(1))


def _mix_call(h, res, wts, batch, seq):
    t, d = h.shape
    tt = MIX_TILE
    n_seq = seq // tt
    row_spec = pl.BlockSpec((tt, d), lambda b, i: (b * n_seq + i, 0))
    slab_spec = pl.BlockSpec((tt * SLAB_ROWS, LANES), lambda b, i: (b * n_seq + i, 0))
    in_specs = ([slab_spec] if res is not None else []) + [row_spec] + [_const_spec(w.shape) for w in wts]
    out_shape = (jax.ShapeDtypeStruct((t, d), f32),
                 jax.ShapeDtypeStruct((t * SLAB_ROWS, LANES), u32),
                 jax.ShapeDtypeStruct((8, t), f32))
    out_specs = (row_spec, slab_spec, pl.BlockSpec((8, tt), lambda b, i: (0, b * n_seq + i)))
    scratch = [pltpu.VMEM((3, tt + 8, LANES), f32), pltpu.VMEM((3, tt, LANES), f32),
               pltpu.VMEM((3, tt, LANES), f32), pltpu.VMEM((2, tt, LANES), f32),
               pltpu.VMEM((2, tt, LANES), f32), pltpu.VMEM((8, 3 * LANES), f32),
               pltpu.VMEM((8, d), f32), pltpu.VMEM((8, d), f32)]
    args = ([res] if res is not None else []) + [h] + list(wts)
    return pl.pallas_call(
        functools.partial(_mix_kernel, has_res=res is not None),
        grid=(batch, n_seq), in_specs=in_specs, out_specs=out_specs,
        out_shape=out_shape, scratch_shapes=scratch, name="mix",
        compiler_params=pltpu.CompilerParams(
            dimension_semantics=("arbitrary", "arbitrary"), vmem_limit_bytes=VMEM_LIMIT),
    )(*args)


def _rank_kernel(rt_ref, tri_ref, br_ref, cnt_ref, carry):
    @pl.when(pl.program_id(0) == 0)
    def _():
        carry[...] = jnp.zeros_like(carry)

    tr = rt_ref.shape[1]
    bk = rt_ref[0:1, :].astype(jnp.int32)
    rows = lax.broadcasted_iota(jnp.int32, (ROUTER_ROWS, tr), 0)
    hit = rows == bk
    oh = jnp.where(hit, 1.0, 0.0)
    before = _dot(oh.astype(bf16), tri_ref[...]) + carry[:, 0:1]
    rank = jnp.sum(jnp.where(hit, before, 0.0), axis=0, keepdims=True).astype(jnp.int32)
    srow = lax.broadcasted_iota(jnp.int32, (8, tr), 0)
    br_ref[...] = jnp.where(srow == 0, bk, rank)
    carry[...] = carry[...] + jnp.sum(oh, axis=1, keepdims=True)
    cnt_ref[...] = carry[...]


def _rank_call(rt, tri):
    t = rt.shape[1]
    tr = tri.shape[0]
    return pl.pallas_call(
        _rank_kernel, grid=(t // tr,),
        in_specs=[pl.BlockSpec((8, tr), lambda i: (0, i)), pl.BlockSpec((tr, tr), lambda i: (0, 0))],
        out_specs=(pl.BlockSpec((8, tr), lambda i: (0, i)),
                   pl.BlockSpec((ROUTER_ROWS, LANES), lambda i: (0, 0))),
        out_shape=(jax.ShapeDtypeStruct((8, t), jnp.int32),
                   jax.ShapeDtypeStruct((ROUTER_ROWS, LANES), f32)),
        scratch_shapes=[pltpu.VMEM((ROUTER_ROWS, LANES), f32)], name="rank",
        compiler_params=pltpu.CompilerParams(dimension_semantics=("arbitrary",)),
    )(rt, tri)


def _invperm_kernel(ps_ref, fill_ref, pe_ref, br_ref, tok_ref):
    step = pl.program_id(0)
    td = br_ref.shape[3]
    n_rows = tok_ref.shape[0]

    @pl.when(step == 0)
    def _():
        def fill(q, carry):
            tok_ref[q] = -1
            return carry

        for k in range(N_BUCKETS):
            lax.fori_loop(fill_ref[k], pe_ref[k], fill, 0)
        lax.fori_loop(pe_ref[N_BUCKETS - 1], n_rows, fill, 0)

    base = step * td

    def place(t, carry):
        tok_ref[ps_ref[br_ref[0, 0, 0, t]] + br_ref[1, 0, 0, t]] = base + t
        return carry

    lax.fori_loop(0, td, place, 0, unroll=8)


def _invperm_call(pad_start, fill_start, pad_end, br4, n_rows):
    n_tiles, td = br4.shape[1], br4.shape[3]
    grid_spec = pltpu.PrefetchScalarGridSpec(
        num_scalar_prefetch=3, grid=(n_tiles,),
        in_specs=[pl.BlockSpec((2, 1, 1, td), lambda i, *_: (0, i, 0, 0), memory_space=pltpu.SMEM)],
        out_specs=pl.BlockSpec(memory_space=pltpu.SMEM))
    return pl.pallas_call(
        _invperm_kernel, grid_spec=grid_spec,
        out_shape=jax.ShapeDtypeStruct((n_rows,), jnp.int32), name="invperm",
        compiler_params=pltpu.CompilerParams(dimension_semantics=("arbitrary",)),
    )(pad_start, fill_start, pad_end, br4)


def _expert_kernel(ea_ref, eb_ref, nv_ref, cur_ref, nxt_ref, xs_ref, wga_ref, wua_ref, wda_ref,
                   wgb_ref, wub_ref, wdb_ref, out_ref, xbuf, ybuf, gsem, ssem, *, n_tok):
    del ea_ref, eb_ref
    blk = cur_ref.shape[2]
    b = pl.program_id(0)
    nv = nv_ref[0]
    slot = lax.rem(b, 2)
    other = 1 - slot

    def gather(tok_ref, to_slot):
        for r in range(blk):
            src = pl.multiple_of(jnp.maximum(tok_ref[0, 0, r], 0) * SLAB_ROWS, SLAB_ROWS)
            pltpu.make_async_copy(xs_ref.at[pl.ds(src, SLAB_ROWS), :],
                                  xbuf.at[to_slot, pl.ds(r * SLAB_ROWS, SLAB_ROWS), :],
                                  gsem.at[to_slot]).start()

    def wait_gather(at_slot):
        pltpu.make_async_copy(xs_ref.at[pl.ds(0, blk * SLAB_ROWS), :], xbuf.at[at_slot],
                              gsem.at[at_slot]).wait()

    def wait_scatter(at_slot):
        pltpu.make_async_copy(ybuf.at[at_slot], out_ref.at[pl.ds(0, blk * SLAB_ROWS), :],
                              ssem.at[at_slot]).wait()

    @pl.when(b == 0)
    def _():
        ybuf[1] = jnp.zeros(ybuf.shape[1:], f32)
        for s in range(2):
            spare = out_ref.at[pl.ds((n_tok + s * blk) * SLAB_ROWS, blk * SLAB_ROWS), :]
            pltpu.make_async_copy(ybuf.at[1], spare, ssem.at[1]).start()
        for s in range(2):
            wait_scatter(1)
        gather(cur_ref, 0)

    @pl.when(b < nv)
    def _():
        wait_gather(slot)

        @pl.when(b >= 2)
        def _():
            wait_scatter(slot)

        gather(nxt_ref, other)

        w = [xbuf[slot, pl.ds(s, blk, stride=SLAB_ROWS), :] for s in range(SLAB_ROWS // 2 + 1)]
        lo = [pltpu.bitcast(w[s] << 16, f32).astype(bf16) for s in range(SLAB_ROWS // 2)]
        hi = [pltpu.bitcast(w[s] & u32(0xFFFF0000), f32).astype(bf16) for s in range(SLAB_ROWS // 2)]
        x = jnp.concatenate(lo + hi, axis=1)
        gates = pltpu.bitcast(w[SLAB_ROWS // 2], f32)
        gate_a = gates[:, 0:1]
        gate_b = gates[:, 1:2]

        def mlp(wg, wu, wd):
            act = jax.nn.silu(_dot(x, wg[...])) * _dot(x, wu[...])
            return _dot(act.astype(bf16), wd[...])

        ys = mlp(wga_ref, wua_ref, wda_ref) * gate_a + mlp(wgb_ref, wub_ref, wdb_ref) * gate_b
        for s in range(SLAB_ROWS):
            ybuf[slot, pl.ds(s, blk, stride=SLAB_ROWS), :] = ys[:, s * LANES:(s + 1) * LANES]

        for r in range(blk):
            tok = cur_ref[0, 0, r]
            dst = pl.multiple_of(
                jnp.where(tok < 0, n_tok + slot * blk + r, tok) * SLAB_ROWS, SLAB_ROWS)
            pltpu.make_async_copy(ybuf.at[slot, pl.ds(r * SLAB_ROWS, SLAB_ROWS), :],
                                  out_ref.at[pl.ds(dst, SLAB_ROWS), :], ssem.at[slot]).start()

        @pl.when(b == nv - 1)
        def _():
            wait_gather(other)
            wait_scatter(slot)

            @pl.when(nv >= 2)
            def _():
                wait_scatter(other)


def _expert_call(ea, eb, nv, tok3, xs, w_gate, w_up, w_down, n_tok):
    nb, _, blk = tok3.shape
    d, de = w_gate.shape[1], w_gate.shape[2]
    wspec = lambda shape, which: pl.BlockSpec(
        (None,) + shape, (lambda b, ea_, eb_, nv_: ((ea_ if which == 0 else eb_)[b], 0, 0)))
    last = lambda b, nv_: jnp.minimum(b, nv_[0] - 1)
    grid_spec = pltpu.PrefetchScalarGridSpec(
        num_scalar_prefetch=3, grid=(nb,),
        in_specs=[pl.BlockSpec((1, 1, blk), lambda b, ea_, eb_, nv_: (last(b, nv_), 0, 0),
                               memory_space=pltpu.SMEM),
                  pl.BlockSpec((1, 1, blk), lambda b, ea_, eb_, nv_: (last(b + 1, nv_), 0, 0),
                               memory_space=pltpu.SMEM),
                  pl.BlockSpec(memory_space=pl.ANY),
                  wspec((d, de), 0), wspec((d, de), 0), wspec((de, d), 0),
                  wspec((d, de), 1), wspec((d, de), 1), wspec((de, d), 1)],
        out_specs=pl.BlockSpec(memory_space=pl.ANY),
        scratch_shapes=[pltpu.VMEM((2, blk * SLAB_ROWS, LANES), u32),
                        pltpu.VMEM((2, blk * SLAB_ROWS, LANES), f32),
                        pltpu.SemaphoreType.DMA((2,)), pltpu.SemaphoreType.DMA((2,))])
    return pl.pallas_call(
        functools.partial(_expert_kernel, n_tok=n_tok), grid_spec=grid_spec,
        out_shape=jax.ShapeDtypeStruct(((n_tok + 2 * blk) * SLAB_ROWS, LANES), f32), name="experts",
        compiler_params=pltpu.CompilerParams(
            dimension_semantics=("arbitrary",), vmem_limit_bytes=VMEM_LIMIT),
    )(ea, eb, nv, tok3, tok3, xs, w_gate, w_up, w_down, w_gate, w_up, w_down)


def _final_kernel(h1_ref, res_ref, g_ref, out_ref):
    out_ref[...] = _rms(h1_ref[...] + _slab_rows(res_ref, h1_ref.shape[0]), g_ref[...])


def _final_call(h1, res, g):
    t, d = h1.shape
    tt = MIX_TILE
    return pl.pallas_call(
        _final_kernel, grid=(t // tt,),
        in_specs=[pl.BlockSpec((tt, d), lambda i: (i, 0)),
                  pl.BlockSpec((tt * SLAB_ROWS, LANES), lambda i: (i, 0)),
                  pl.BlockSpec((1, d), lambda i: (0, 0))],
        out_specs=pl.BlockSpec((tt, d), lambda i: (i, 0)),
        out_shape=jax.ShapeDtypeStruct((t, d), f32), name="final",
        compiler_params=pltpu.CompilerParams(dimension_semantics=("arbitrary",)),
    )(h1, res, g)


def _cmul(a, b):
    return a[0] * b[0] - a[1] * b[1], a[0] * b[1] + a[1] * b[0]


def _s5_tables(a_re, a_im, log_dt, b_re, b_im, c_re, c_im, n):
    groups, states = a_re.shape
    chans = b_re.shape[2]
    steps = TOKENS_PER_ROW
    gh = groups // 2
    dt = jnp.exp(log_dt)[:, None]
    mag = jnp.exp(a_re * dt)
    lbar = (mag * jnp.cos(a_im * dt), mag * jnp.sin(a_im * dt))
    den = a_re * a_re + a_im * a_im
    qr = ((lbar[0] - 1.0) * a_re + lbar[1] * a_im) / den
    qi = (lbar[1] * a_re - (lbar[0] - 1.0) * a_im) / den
    bbar = _cmul((qr[:, :, None], qi[:, :, None]), (b_re, b_im))
    pw = [(jnp.ones_like(mag), jnp.zeros_like(mag))]
    for _ in range(steps):
        pw.append(_cmul(pw[-1], lbar))
    eye = jnp.eye(gh, dtype=f32)
    col = lambda p: (p[0][:, :, None], p[1][:, :, None])

    wb = jnp.stack([jnp.stack(_cmul(col(pw[steps - 1 - s]), bbar), 0) for s in range(steps)], 0)
    lagk = []
    for lag in range(steps):
        lb = _cmul(col(pw[lag]), bbar)
        lagk.append(jnp.sum(c_re[:, :, :, None] * lb[0][:, None, :, :]
                            - c_im[:, :, :, None] * lb[1][:, None, :, :], axis=2))
    lagk = jnp.stack(lagk, 0)
    s_idx = jnp.arange(steps)[:, None]
    t_idx = jnp.arange(steps)[None, :]
    toe = jnp.where((s_idx <= t_idx)[:, :, None, None, None],
                    lagk[jnp.clip(t_idx - s_idx, 0, steps - 1)], 0.0)
    wc = []
    for t in range(steps):
        cl = _cmul((c_re, c_im), (pw[t + 1][0][:, None, :], pw[t + 1][1][:, None, :]))
        wc.append(jnp.stack([cl[0], -cl[1]], 0))
    wc = jnp.stack(wc, 0)

    s5a, s5c = [], []
    for hh in range(2):
        gs = slice(hh * gh, (hh + 1) * gh)
        wb_h = jnp.einsum("ab,srapc->sacrbp", eye, wb[:, :, gs]).reshape(steps * gh * chans, 2 * gh * states)
        toe_h = jnp.einsum("ab,stacd->sadtbc", eye, toe[:, :, gs]).reshape(steps * gh * chans, steps * gh * chans)
        wc_h = jnp.einsum("ab,tracp->raptbc", eye, wc[:, :, gs]).reshape(2 * gh * states, steps * gh * chans)
        s5a.append(jnp.concatenate([wb_h, toe_h], axis=1))
        s5c.append(wc_h)
    s5a = jnp.stack(s5a, 0).astype(bf16)
    s5c = jnp.stack(s5c, 0).astype(bf16)

    flat = lambda p: (p[0].reshape(1, groups * states), p[1].reshape(1, groups * states))
    m = flat(pw[steps])
    pk = [m]
    for _ in range(n.bit_length() - 2):
        pk.append(_cmul(pk[-1], pk[-1]))
    pj = m
    while pj[0].shape[0] < n:
        top = (pj[0][-1:], pj[1][-1:])
        nxt = _cmul(pj, top)
        pj = (jnp.concatenate([pj[0], nxt[0]], 0), jnp.concatenate([pj[1], nxt[1]], 0))
    pkr = jnp.concatenate([p[0] for p in pk], 0)
    pki = jnp.concatenate([p[1] for p in pk], 0)
    return s5a, s5c, pkr, pki, pj[0], pj[1]


def _layer_weights(l, p, n):
    row = lambda x: x.reshape(1, -1).astype(f32)
    heads = p["a_spatial_w"].shape[1]
    cid = jnp.arange(A_BLOCK) // CHUNK
    aw = jnp.where((cid[None, :] <= cid[:, None])[None], p["a_spatial_w"][l], 0.0).astype(bf16)
    ab = jnp.repeat(p["a_spatial_b"][l].T, HEAD_DIM, axis=1)
    eye_h = jnp.eye(heads, dtype=f32)
    wg = jnp.concatenate([jnp.einsum("ab,aij->aibj", eye_h, p["b_rg_w"][l]).reshape(heads * HEAD_DIM, -1),
                          jnp.einsum("ab,aij->aibj", eye_h, p["b_ig_w"][l]).reshape(heads * HEAD_DIM, -1)],
                         axis=1).astype(bf16)
    bg = jnp.concatenate([p["b_rg_b"][l], p["b_ig_b"][l]]).reshape(1, -1)
    nl = row(-LRU_C * jax.nn.softplus(-p["b_lambda"][l]))
    s5a, s5c, pkr, pki, pjr, pji = _s5_tables(
        p["c_a_re"][l], p["c_a_im"][l], p["c_log_dt"][l], p["c_b_re"][l], p["c_b_im"][l],
        p["c_c_re"][l], p["c_c_im"][l], n)
    d = p["w_in"].shape[1]
    wr = jnp.concatenate([p["router_group_w"][l].T,
                          jnp.transpose(p["router_expert_w"][l], (0, 2, 1)).reshape(-1, d)], 0)
    wr = jnp.concatenate([wr, jnp.zeros((ROUTER_ROWS - wr.shape[0], d), f32)], 0)
    wr_hi = wr.astype(bf16)
    wr_lo = (wr - wr_hi.astype(f32)).astype(bf16)
    rb = jnp.concatenate([p["router_group_b"][l], p["router_expert_b"][l].reshape(-1)])
    rb = jnp.concatenate([rb, jnp.zeros((ROUTER_ROWS - rb.shape[0],), f32)]).reshape(-1, 1)
    return [row(p["norm_mix_g"][l]), p["w_in"][l].astype(bf16), row(p["a_v_norm_g"][l]), aw, ab,
            p["b_conv_w"][l], row(p["b_conv_b"][l]), wg, bg, nl, s5a, s5c, pkr, pki, pjr, pji,
            row(p["c_d"][l]), p["c_glu_w"][l].astype(bf16), row(p["c_glu_b"][l]),
            row(p["mix_out_norm_g"][l]), p["w_out"][l].astype(bf16), row(p["norm_ffn_g"][l]),
            jnp.concatenate([wr_hi, wr_lo], 0), wr_hi, rb]


def _moe(xs, rt, rank_tri, w_gate, w_up, w_down, first_expert):
    t = rt.shape[1]
    br8, cnt = _rank_call(rt, rank_tri)
    counts = cnt[:N_BUCKETS, 0].astype(jnp.int32)
    padded = ((counts + MOE_BLOCK - 1) // MOE_BLOCK) * MOE_BLOCK
    pad_end = jnp.cumsum(padded)
    pad_start = pad_end - padded
    n_blocks = t // MOE_BLOCK + N_BUCKETS
    n_rows = n_blocks * MOE_BLOCK
    br4 = br8[0:2].reshape(2, t // INV_TILE, 1, INV_TILE)
    tok = _invperm_call(pad_start, pad_start + counts, pad_end, br4, n_rows)
    nv = (pad_end[-1] // MOE_BLOCK).astype(jnp.int32)
    blk_first = jnp.minimum(jnp.arange(n_blocks, dtype=jnp.int32), nv - 1) * MOE_BLOCK
    blk_bucket = jnp.minimum(
        jnp.sum((pad_end[None, :] <= blk_first[:, None]).astype(jnp.int32), axis=1), N_BUCKETS - 1)
    lo_tab = jnp.array([0, 0, 0, 1, 1, 2], jnp.int32)
    hi_tab = jnp.array([1, 2, 3, 2, 3, 3], jnp.int32)
    grp, pr = blk_bucket // N_PAIRS + first_expert // 4, blk_bucket % N_PAIRS
    ea = grp * 4 + jnp.sum(jnp.where(pr[:, None] == jnp.arange(N_PAIRS)[None, :], lo_tab[None, :], 0), axis=1)
    eb = grp * 4 + jnp.sum(jnp.where(pr[:, None] == jnp.arange(N_PAIRS)[None, :], hi_tab[None, :], 0), axis=1)
    return _expert_call(ea, eb, nv.reshape(1), tok.reshape(n_blocks, 1, MOE_BLOCK), xs,
                        w_gate, w_up, w_down, t)


def kernel(x, norm_mix_g, w_in, a_v_norm_g, a_spatial_w, a_spatial_b, b_conv_w, b_conv_b, b_rg_w, b_rg_b, b_ig_w, b_ig_b, b_lambda, c_a_re, c_a_im, c_log_dt, c_b_re, c_b_im, c_c_re, c_c_im, c_d, c_glu_w, c_glu_b, mix_out_norm_g, w_out, norm_ffn_g, router_group_w, router_group_b, router_expert_w, router_expert_b, expert_w_gate, expert_w_up, expert_w_down, final_norm_g):
    p = dict(locals())
    batch, seq, d = x.shape
    depth = w_in.shape[0]
    t = batch * seq
    assert d == SLAB_ROWS * LANES and seq % MIX_TILE == 0 and t % RANK_TILE == 0 and t % INV_TILE == 0
    n = MIX_TILE // TOKENS_PER_ROW
    idx = jnp.arange(RANK_TILE)
    rank_tri = (idx[:, None] < idx[None, :]).astype(bf16)
    n_exp = expert_w_gate.shape[1]
    wg_all, wu_all, wd_all = (w.astype(bf16).reshape((depth * n_exp,) + w.shape[2:])
                              for w in (expert_w_gate, expert_w_up, expert_w_down))
    h = x.reshape(t, d)
    res = None
    for l in range(depth):
        h, xs, rt = _mix_call(h, res, _layer_weights(l, p, n), batch, seq)
        res = _moe(xs, rt, rank_tri, wg_all, wu_all, wd_all, l * n_exp)
    return _final_call(h, res, final_norm_g.reshape(1, d)).reshape(batch, seq, d)
```

```python
import functools

import jax
import jax.numpy as jnp
from jax import lax
from jax.experimental import pallas as pl
from jax.experimental.pallas import tpu as pltpu

EPS = 1e-6
CHUNK = 64
A_BLOCK = 128
HEAD_DIM = 64
LRU_C = 8.0
TOKENS_PER_ROW = 8
SLAB_ROWS = 8
LANES = 128
N_PAIRS = 6
N_BUCKETS = 4 * N_PAIRS
MIX_TILE = 512
MOE_BLOCK = 256
INV_TILE = 2048
ROUTER_ROWS = 32
VMEM_LIMIT = 56 * 1024 * 1024

f32 = jnp.float32
bf16 = jnp.bfloat16
u32 = jnp.uint32


def _rms(x, g):
    ms = jnp.mean(x * x, axis=-1, keepdims=True)
    return x * lax.rsqrt(ms + EPS) * g


def _dot(a, b):
    return jnp.dot(a, b, preferred_element_type=f32)


def _dot_nt(a, b):
    return lax.dot_general(a, b, (((1,), (1,)), ((), ())), preferred_element_type=f32)


def _slab_rows(ref, rows):
    return jnp.concatenate([ref[pl.ds(s, rows, stride=SLAB_ROWS), :] for s in range(SLAB_ROWS)], axis=1)


def _mix_kernel(*refs, has_res):
    if has_res:
        _mix_body(*refs[1:], res_ref=refs[0])
    else:
        _mix_body(*refs, res_ref=None)


def _mix_body(h_ref, ng_ref, win_ref, avg_ref, aw_ref, ab_ref, cw_ref, cb_ref, wg_ref, bg_ref,
              nl_ref, s5a_ref, s5c_ref, pkr_ref, pki_ref, pjr_ref, pji_ref, cd_ref, gw_ref,
              gb_ref, og_ref, wout_ref, fg_ref, wr_ref, wrhi_ref, rb_ref, tri_ref,
              h1_ref, xs_ref, br_ref, cnt_ref,
              xb, gbuf, ybuf, zc, yc, lru_c, s5r_c, s5i_c, cnt_c, *, res_ref):
    tt = h_ref.shape[0]
    n = tt // TOKENS_PER_ROW
    n_steps = n.bit_length() - 1
    aw = A_BLOCK * 3
    off_av, off_bx, off_bg, off_cx = aw, 2 * aw, 3 * aw, 4 * aw

    @pl.when((pl.program_id(0) == 0) & (pl.program_id(1) == 0))
    def _():
        cnt_c[...] = jnp.zeros_like(cnt_c)

    @pl.when(pl.program_id(1) == 0)
    def _():
        xb[:, 0:8, :] = jnp.zeros((3, 8, LANES), f32)
        lru_c[...] = jnp.zeros_like(lru_c)
        s5r_c[...] = jnp.zeros_like(s5r_c)
        s5i_c[...] = jnp.zeros_like(s5i_c)

    h = h_ref[...]
    if res_ref is not None:
        h = h + _slab_rows(res_ref, tt)
    xn = _rms(h, ng_ref[...]).astype(bf16)
    z = _dot(xn, win_ref[...])

    u = jax.nn.gelu(z[:, 0:aw])
    v = _rms(jax.nn.gelu(z[:, off_av:off_av + aw]), avg_ref[...]).astype(bf16)
    nblk = tt // A_BLOCK
    low_half = (lax.broadcasted_iota(jnp.int32, (A_BLOCK, A_BLOCK * nblk), 1) & (LANES - 1)) < HEAD_DIM
    sv_pairs = []
    for p in range(3):
        rhs = jnp.concatenate(
            [v[b * A_BLOCK:(b + 1) * A_BLOCK, p * LANES:(p + 1) * LANES] for b in range(nblk)], axis=1)
        r0 = _dot(aw_ref[2 * p], rhs)
        r1 = _dot(aw_ref[2 * p + 1], rhs)
        sv_pairs.append(jnp.where(low_half, r0, r1))
    sv = jnp.concatenate(
        [jnp.concatenate([sv_pairs[p][:, b * LANES:(b + 1) * LANES] for p in range(3)], axis=1)
         for b in range(nblk)], axis=0)
    bias_a = jnp.concatenate([ab_ref[...]] * nblk, axis=0)
    na = _rms(u * (sv + bias_a), og_ref[:, 0:aw])

    for j in range(3):
        xb[j, 8:tt + 8, :] = z[:, off_bx + j * LANES:off_bx + (j + 1) * LANES]
        gbuf[j, :, :] = z[:, off_bg + j * LANES:off_bg + (j + 1) * LANES]
    phases = []
    for s in range(TOKENS_PER_ROW):
        cols = []
        for j in range(3):
            acc = cb_ref[:, j * LANES:(j + 1) * LANES] + cw_ref[3:4, j * LANES:(j + 1) * LANES] * \
                xb[j, pl.ds(8 + s, n, stride=TOKENS_PER_ROW), :]
            for k in range(3):
                acc = acc + cw_ref[k:k + 1, j * LANES:(j + 1) * LANES] * \
                    xb[j, pl.ds(5 + s + k, n, stride=TOKENS_PER_ROW), :]
            cols.append(acc)
        phases.append(jnp.concatenate(cols, axis=1))
    xc = jnp.concatenate(phases, axis=0)
    ri = _dot(xc.astype(bf16), wg_ref[...]) + bg_ref[...]
    log_a = nl_ref[...] * jax.nn.sigmoid(ri[:, 0:aw])
    a_all = jnp.exp(log_a)
    b_all = jnp.sqrt(-jnp.tanh(log_a) * (a_all * a_all + 1.0)) * (jax.nn.sigmoid(ri[:, aw:2 * aw]) * xc)
    a_cum, h_loc = [], []
    for s in range(TOKENS_PER_ROW):
        a = a_all[s * n:(s + 1) * n, :]
        b = b_all[s * n:(s + 1) * n, :]
        if s == 0:
            a_cum.append(a)
            h_loc.append(b)
        else:
            h_loc.append(a * h_loc[-1] + b)
            a_cum.append(a * a_cum[-1])
    row = lax.broadcasted_iota(jnp.int32, (n, 1), 0)
    ac, hc = a_cum[-1], h_loc[-1]
    for k in range(n_steps):
        sh = 1 << k
        keep = row >= sh
        ar = pltpu.roll(ac, sh, 0)
        hr = pltpu.roll(hc, sh, 0)
        hc = jnp.where(keep, ac * hr + hc, hc)
        ac = jnp.where(keep, ac * ar, ac)
    cin = lru_c[0:1, :]
    incl = hc + ac * cin
    c0 = jnp.where(row == 0, cin, pltpu.roll(incl, 1, 0))
    lru_c[0:1, :] = incl[n - 1:n, :]
    for s in range(TOKENS_PER_ROW):
        gate = jnp.concatenate(
            [gbuf[j, pl.ds(s, n, stride=TOKENS_PER_ROW), :] for j in range(3)], axis=1)
        yb = (h_loc[s] + a_cum[s] * c0) * jax.nn.gelu(gate)
        for j in range(3):
            ybuf[j, pl.ds(s, n, stride=TOKENS_PER_ROW), :] = yb[:, j * LANES:(j + 1) * LANES]
    xb[:, 0:8, :] = xb[:, tt:tt + 8, :]
    nb = _rms(jnp.concatenate([ybuf[j] for j in range(3)], axis=1), og_ref[:, aw:2 * aw])

    for hh in range(2):
        zc[hh, :, :] = z[:, off_cx + hh * LANES:off_cx + (hh + 1) * LANES]
    half = 4 * LANES
    e_parts, y_intra = [], []
    for hh in range(2):
        xrow = jnp.concatenate(
            [zc[hh, pl.ds(s, n, stride=TOKENS_PER_ROW), :] for s in range(TOKENS_PER_ROW)],
            axis=1).astype(bf16)
        res = _dot(xrow, s5a_ref[hh])
        e_parts.append(res[:, 0:2 * half])
        y_intra.append(res[:, 2 * half:4 * half])
    er = jnp.concatenate([e_parts[0][:, 0:half], e_parts[1][:, 0:half]], axis=1)
    ei = jnp.concatenate([e_parts[0][:, half:2 * half], e_parts[1][:, half:2 * half]], axis=1)
    for k in range(n_steps):
        sh = 1 << k
        keep = row >= sh
        pr = pkr_ref[k:k + 1, :]
        pi = pki_ref[k:k + 1, :]
        rr = pltpu.roll(er, sh, 0)
        rim = pltpu.roll(ei, sh, 0)
        er, ei = (er + jnp.where(keep, pr * rr - pi * rim, 0.0),
                  ei + jnp.where(keep, pr * rim + pi * rr, 0.0))
    cr = s5r_c[0:1, :]
    ci = s5i_c[0:1, :]
    fr = er + pjr_ref[...] * cr - pji_ref[...] * ci
    fi = ei + pjr_ref[...] * ci + pji_ref[...] * cr
    s0r = jnp.where(row == 0, cr, pltpu.roll(fr, 1, 0))
    s0i = jnp.where(row == 0, ci, pltpu.roll(fi, 1, 0))
    s5r_c[0:1, :] = fr[n - 1:n, :]
    s5i_c[0:1, :] = fi[n - 1:n, :]
    for hh in range(2):
        s0 = jnp.concatenate(
            [s0r[:, hh * half:(hh + 1) * half], s0i[:, hh * half:(hh + 1) * half]],
            axis=1).astype(bf16)
        yrow = y_intra[hh] + _dot(s0, s5c_ref[hh])
        for t in range(TOKENS_PER_ROW):
            yc[hh, pl.ds(t, n, stride=TOKENS_PER_ROW), :] = yrow[:, t * LANES:(t + 1) * LANES]
    ytok = jnp.concatenate([yc[0], yc[1]], axis=1) + cd_ref[...] * z[:, off_cx:off_cx + 2 * LANES]
    yg = jax.nn.gelu(ytok)
    ycm = yg * jax.nn.sigmoid(_dot(yg.astype(bf16), gw_ref[...]) + gb_ref[...])
    nc = _rms(ycm, og_ref[:, 2 * aw:2 * aw + 2 * LANES])

    ycat = jnp.concatenate([na, nb, nc], axis=1).astype(bf16)
    h1 = h + _dot(ycat, wout_ref[...])
    h1_ref[...] = h1
    hn = _rms(h1, fg_ref[...])

    hn_hi = hn.astype(bf16)
    hn_lo = (hn - hn_hi.astype(f32)).astype(bf16)
    part = _dot_nt(wr_ref[...], hn_hi)
    lt = (part[0:ROUTER_ROWS] + part[ROUTER_ROWS:2 * ROUTER_ROWS]
          + _dot_nt(wrhi_ref[...], hn_lo) + rb_ref[...])
    g = [lt[k:k + 1, :] for k in range(4)]
    gmax = jnp.maximum(jnp.maximum(g[0], g[1]), jnp.maximum(g[2], g[3]))
    sel = jnp.where(g[0] == gmax, 0, jnp.where(g[1] == gmax, 1, jnp.where(g[2] == gmax, 2, 3)))
    g_w = 1.0 / (jnp.exp(g[0] - gmax) + jnp.exp(g[1] - gmax) + jnp.exp(g[2] - gmax) + jnp.exp(g[3] - gmax))
    e = []
    for k in range(4):
        e.append(jnp.where(sel == 0, lt[4 + k:5 + k, :],
                           jnp.where(sel == 1, lt[8 + k:9 + k, :],
                                     jnp.where(sel == 2, lt[12 + k:13 + k, :], lt[16 + k:17 + k, :]))))
    v1 = jnp.maximum(jnp.maximum(e[0], e[1]), jnp.maximum(e[2], e[3]))
    i1 = jnp.where(e[0] == v1, 0, jnp.where(e[1] == v1, 1, jnp.where(e[2] == v1, 2, 3)))
    neg = -jnp.inf
    m = [jnp.where(i1 == k, neg, e[k]) for k in range(4)]
    v2 = jnp.maximum(jnp.maximum(m[0], m[1]), jnp.maximum(m[2], m[3]))
    i2 = jnp.where(m[0] == v2, 0, jnp.where(m[1] == v2, 1, jnp.where(m[2] == v2, 2, 3)))
    t2 = jnp.exp(v2 - v1)
    w1 = 1.0 / (1.0 + t2)
    w2 = t2 / (1.0 + t2)
    first_low = i1 < i2
    lo = jnp.where(first_low, i1, i2)
    hi = jnp.where(first_low, i2, i1)
    gate_a = g_w * jnp.where(first_low, w1, w2)
    gate_b = g_w * jnp.where(first_low, w2, w1)
    pair = jnp.where(lo == 0, 0, jnp.where(lo == 1, 3, 5)) + hi - lo - 1
    bucket = sel * N_PAIRS + pair

    hit = lax.broadcasted_iota(jnp.int32, (ROUTER_ROWS, tt), 0) == bucket
    onehot = jnp.where(hit, 1.0, 0.0)
    before = _dot(onehot.astype(bf16), tri_ref[...]) + cnt_c[:, 0:1]
    rank = jnp.sum(jnp.where(hit, before, 0.0), axis=0, keepdims=True).astype(jnp.int32)
    srow = lax.broadcasted_iota(jnp.int32, (8, tt), 0)
    br_ref[...] = jnp.where(srow == 0, bucket, rank)
    cnt_c[...] = cnt_c[...] + jnp.sum(onehot, axis=1, keepdims=True)
    cnt_ref[...] = cnt_c[...]

    lrow = lax.broadcasted_iota(jnp.int32, (LANES, tt), 0)
    gates_t = jnp.where(lrow == 0, gate_a, jnp.where(lrow == 1, gate_b, 0.0))

    bits = pltpu.bitcast(hn_hi.astype(f32), u32)
    dh = hn.shape[1] // 2
    words = (bits[:, 0:dh] >> 16) | (bits[:, dh:2 * dh] & u32(0xFFFF0000))
    feat_rows = dh // LANES
    for s in range(feat_rows):
        xs_ref[pl.ds(s, tt, stride=SLAB_ROWS), :] = words[:, s * LANES:(s + 1) * LANES]
    xs_ref[pl.ds(feat_rows, tt, stride=SLAB_ROWS), :] = pltpu.bitcast(gates_t.T, u32)
    for s in range(feat_rows + 1, SLAB_ROWS):
        xs_ref[pl.ds(s, tt, stride=SLAB_ROWS), :] = jnp.zeros((tt, LANES), u32)


def _const_spec(shape):
    nd = len(shape)
    return pl.BlockSpec(shape, lambda b, i: (0,) * nd, pipeline_mode=pl.Buffered(1))


def _mix_call(h, res, wts, batch, seq):
    t, d = h.shape
    tt = MIX_TILE
    n_seq = seq // tt
    row_spec = pl.BlockSpec((tt, d), lambda b, i: (b * n_seq + i, 0))
    slab_spec = pl.BlockSpec((tt * SLAB_ROWS, LANES), lambda b, i: (b * n_seq + i, 0))
    in_specs = ([slab_spec] if res is not None else []) + [row_spec] + [_const_spec(w.shape) for w in wts]
    out_shape = (jax.ShapeDtypeStruct((t, d), f32),
                 jax.ShapeDtypeStruct((t * SLAB_ROWS, LANES), u32),
                 jax.ShapeDtypeStruct((8, t), jnp.int32),
                 jax.ShapeDtypeStruct((ROUTER_ROWS, LANES), f32))
    out_specs = (row_spec, slab_spec, pl.BlockSpec((8, tt), lambda b, i: (0, b * n_seq + i)),
                 pl.BlockSpec((ROUTER_ROWS, LANES), lambda b, i: (0, 0)))
    scratch = [pltpu.VMEM((3, tt + 8, LANES), f32), pltpu.VMEM((3, tt, LANES), f32),
               pltpu.VMEM((3, tt, LANES), f32), pltpu.VMEM((2, tt, LANES), f32),
               pltpu.VMEM((2, tt, LANES), f32), pltpu.VMEM((8, 3 * LANES), f32),
               pltpu.VMEM((8, d), f32), pltpu.VMEM((8, d), f32),
               pltpu.VMEM((ROUTER_ROWS, LANES), f32)]
    args = ([res] if res is not None else []) + [h] + list(wts)
    return pl.pallas_call(
        functools.partial(_mix_kernel, has_res=res is not None),
        grid=(batch, n_seq), in_specs=in_specs, out_specs=out_specs,
        out_shape=out_shape, scratch_shapes=scratch, name="mix",
        compiler_params=pltpu.CompilerParams(
            dimension_semantics=("arbitrary", "arbitrary"), vmem_limit_bytes=VMEM_LIMIT),
    )(*args)


def _invperm_kernel(fill_ref, pe_ref, dest_ref, tok_ref, *, n_tok, n_spare):
    step = pl.program_id(0)
    td = dest_ref.shape[0]
    n_rows = tok_ref.shape[0]

    @pl.when(step == 0)
    def _():
        def fill(q, carry):
            tok_ref[q] = n_tok + (q & (n_spare - 1))
            return carry

        for k in range(N_BUCKETS):
            lax.fori_loop(fill_ref[k], pe_ref[k], fill, 0)
        lax.fori_loop(pe_ref[N_BUCKETS - 1], n_rows, fill, 0)

    base = step * td

    def place(t, carry):
        tok_ref[dest_ref[t]] = base + t
        return carry

    lax.fori_loop(0, td, place, 0, unroll=16)


def _invperm_call(fill_start, pad_end, dest, n_rows, n_spare):
    t = dest.shape[0]
    grid_spec = pltpu.PrefetchScalarGridSpec(
        num_scalar_prefetch=2, grid=(t // INV_TILE,),
        in_specs=[pl.BlockSpec((INV_TILE,), lambda i, *_: (i,), memory_space=pltpu.SMEM)],
        out_specs=pl.BlockSpec(memory_space=pltpu.SMEM))
    return pl.pallas_call(
        functools.partial(_invperm_kernel, n_tok=t, n_spare=n_spare), grid_spec=grid_spec,
        out_shape=jax.ShapeDtypeStruct((n_rows,), jnp.int32), name="invperm",
        compiler_params=pltpu.CompilerParams(dimension_semantics=("arbitrary",)),
    )(fill_start, pad_end, dest)


def _expert_kernel(ea_ref, eb_ref, nv_ref, prv_ref, cur_ref, nxt_ref, xs_ref, wga_ref, wua_ref,
                   wda_ref, wgb_ref, wub_ref, wdb_ref, out_ref, xbuf, ybuf, gsem, ssem, *, n_tok):
    del ea_ref, eb_ref
    blk = cur_ref.shape[2]
    n_groups = 16
    per_group = blk // n_groups
    b = pl.program_id(0)
    nv = nv_ref[0]
    slot = lax.rem(b, 2)
    other = 1 - slot

    def gather_rows(tok_ref, to_slot, rows):
        for r in rows:
            src = pl.multiple_of(jnp.minimum(tok_ref[0, 0, r], n_tok - 1) * SLAB_ROWS, SLAB_ROWS)
            pltpu.make_async_copy(xs_ref.at[pl.ds(src, SLAB_ROWS), :],
                                  xbuf.at[to_slot, pl.ds(r * SLAB_ROWS, SLAB_ROWS), :],
                                  gsem.at[to_slot]).start()

    def scatter_rows(tok_ref, from_slot, rows):
        for r in rows:
            dst = pl.multiple_of(tok_ref[0, 0, r] * SLAB_ROWS, SLAB_ROWS)
            pltpu.make_async_copy(ybuf.at[from_slot, pl.ds(r * SLAB_ROWS, SLAB_ROWS), :],
                                  out_ref.at[pl.ds(dst, SLAB_ROWS), :], ssem.at[from_slot]).start()

    def wait_gather(at_slot):
        pltpu.make_async_copy(xs_ref.at[pl.ds(0, blk * SLAB_ROWS), :], xbuf.at[at_slot],
                              gsem.at[at_slot]).wait()

    def wait_scatter(at_slot):
        pltpu.make_async_copy(ybuf.at[at_slot], out_ref.at[pl.ds(0, blk * SLAB_ROWS), :],
                              ssem.at[at_slot]).wait()

    @pl.when(b == 0)
    def _():
        ybuf[1] = jnp.zeros(ybuf.shape[1:], f32)
        for s in range(2):
            spare = out_ref.at[pl.ds((n_tok + s * blk) * SLAB_ROWS, blk * SLAB_ROWS), :]
            pltpu.make_async_copy(ybuf.at[1], spare, ssem.at[1]).start()
        for s in range(2):
            wait_scatter(1)
        gather_rows(cur_ref, 0, range(blk))

    def step(scatter_prev):
        wait_gather(slot)
        if scatter_prev:
            @pl.when(b >= 2)
            def _():
                wait_scatter(slot)

        def dma_group(k):
            rows = range(k * per_group, (k + 1) * per_group)
            gather_rows(nxt_ref, other, rows)
            if scatter_prev:
                scatter_rows(prv_ref, other, rows)

        w = [xbuf[slot, pl.ds(s, blk, stride=SLAB_ROWS), :] for s in range(SLAB_ROWS // 2 + 1)]
        lo = [pltpu.bitcast(w[s] << 16, f32).astype(bf16) for s in range(SLAB_ROWS // 2)]
        hi = [pltpu.bitcast(w[s] & u32(0xFFFF0000), f32).astype(bf16) for s in range(SLAB_ROWS // 2)]
        x = jnp.concatenate(lo + hi, axis=1)
        gates = pltpu.bitcast(w[SLAB_ROWS // 2], f32)
        cw = 2 * LANES
        group = 0
        first = []
        for e, (wg, wu, wd, gate) in enumerate(((wga_ref, wua_ref, wda_ref, gates[:, 0:1]),
                                                (wgb_ref, wub_ref, wdb_ref, gates[:, 1:2]))):
            acts = []
            for c in range(wg.shape[1] // cw):
                gc = _dot(x, wg[:, c * cw:(c + 1) * cw])
                dma_group(group)
                uc = _dot(x, wu[:, c * cw:(c + 1) * cw])
                dma_group(group + 1)
                group += 2
                acts.append((jax.nn.silu(gc) * uc).astype(bf16))
            act = jnp.concatenate(acts, axis=1)
            for c in range(wd.shape[1] // cw):
                yc = _dot(act, wd[:, c * cw:(c + 1) * cw]) * gate
                dma_group(group)
                group += 1
                if e == 0:
                    first.append(yc)
                else:
                    ys = first[c] + yc
                    for s in range(cw // LANES):
                        ybuf[slot, pl.ds(c * (cw // LANES) + s, blk, stride=SLAB_ROWS), :] = \
                            ys[:, s * LANES:(s + 1) * LANES]
        assert group == n_groups

        @pl.when(b == nv - 1)
        def _():
            scatter_rows(cur_ref, slot, range(blk))
            wait_gather(other)
            wait_scatter(slot)
            if scatter_prev:
                wait_scatter(other)

    @pl.when(b == 0)
    def _():
        step(False)

    @pl.when((b > 0) & (b < nv))
    def _():
        step(True)


def _expert_call(ea, eb, nv, tok3, xs, w_gate, w_up, w_down, n_tok):
    nb, _, blk = tok3.shape
    d, de = w_gate.shape[1], w_gate.shape[2]
    wspec = lambda shape, which: pl.BlockSpec(
        (None,) + shape, (lambda b, ea_, eb_, nv_: ((ea_ if which == 0 else eb_)[b], 0, 0)))
    last = lambda b, nv_: jnp.minimum(b, nv_[0] - 1)
    grid_spec = pltpu.PrefetchScalarGridSpec(
        num_scalar_prefetch=3, grid=(nb,),
        in_specs=[pl.BlockSpec((1, 1, blk), lambda b, ea_, eb_, nv_: (last(jnp.maximum(b - 1, 0), nv_), 0, 0),
                               memory_space=pltpu.SMEM),
                  pl.BlockSpec((1, 1, blk), lambda b, ea_, eb_, nv_: (last(b, nv_), 0, 0),
                               memory_space=pltpu.SMEM),
                  pl.BlockSpec((1, 1, blk), lambda b, ea_, eb_, nv_: (last(b + 1, nv_), 0, 0),
                               memory_space=pltpu.SMEM),
                  pl.BlockSpec(memory_space=pl.ANY),
                  wspec((d, de), 0), wspec((d, de), 0), wspec((de, d), 0),
                  wspec((d, de), 1), wspec((d, de), 1), wspec((de, d), 1)],
        out_specs=pl.BlockSpec(memory_space=pl.ANY),
        scratch_shapes=[pltpu.VMEM((2, blk * SLAB_ROWS, LANES), u32),
                        pltpu.VMEM((2, blk * SLAB_ROWS, LANES), f32),
                        pltpu.SemaphoreType.DMA((2,)), pltpu.SemaphoreType.DMA((2,))])
    return pl.pallas_call(
        functools.partial(_expert_kernel, n_tok=n_tok), grid_spec=grid_spec,
        out_shape=jax.ShapeDtypeStruct(((n_tok + 2 * blk) * SLAB_ROWS, LANES), f32), name="experts",
        compiler_params=pltpu.CompilerParams(
            dimension_semantics=("arbitrary",), vmem_limit_bytes=VMEM_LIMIT),
    )(ea, eb, nv, tok3, tok3, tok3, xs, w_gate, w_up, w_down, w_gate, w_up, w_down)


def _final_kernel(h1_ref, res_ref, g_ref, out_ref):
    out_ref[...] = _rms(h1_ref[...] + _slab_rows(res_ref, h1_ref.shape[0]), g_ref[...])


def _final_call(h1, res, g):
    t, d = h1.shape
    tt = MIX_TILE
    return pl.pallas_call(
        _final_kernel, grid=(t // tt,),
        in_specs=[pl.BlockSpec((tt, d), lambda i: (i, 0)),
                  pl.BlockSpec((tt * SLAB_ROWS, LANES), lambda i: (i, 0)),
                  pl.BlockSpec((1, d), lambda i: (0, 0))],
        out_specs=pl.BlockSpec((tt, d), lambda i: (i, 0)),
        out_shape=jax.ShapeDtypeStruct((t, d), f32), name="final",
        compiler_params=pltpu.CompilerParams(dimension_semantics=("arbitrary",)),
    )(h1, res, g)


def _cmul(a, b):
    return a[0] * b[0] - a[1] * b[1], a[0] * b[1] + a[1] * b[0]


def _s5_tables(a_re, a_im, log_dt, b_re, b_im, c_re, c_im, n):
    groups, states = a_re.shape
    chans = b_re.shape[2]
    steps = TOKENS_PER_ROW
    gh = groups // 2
    dt = jnp.exp(log_dt)[:, None]
    mag = jnp.exp(a_re * dt)
    lbar = (mag * jnp.cos(a_im * dt), mag * jnp.sin(a_im * dt))
    den = a_re * a_re + a_im * a_im
    qr = ((lbar[0] - 1.0) * a_re + lbar[1] * a_im) / den
    qi = (lbar[1] * a_re - (lbar[0] - 1.0) * a_im) / den
    bbar = _cmul((qr[:, :, None], qi[:, :, None]), (b_re, b_im))
    pw = [(jnp.ones_like(mag), jnp.zeros_like(mag))]
    for _ in range(steps):
        pw.append(_cmul(pw[-1], lbar))
    eye = jnp.eye(gh, dtype=f32)
    col = lambda p: (p[0][:, :, None], p[1][:, :, None])

    wb = jnp.stack([jnp.stack(_cmul(col(pw[steps - 1 - s]), bbar), 0) for s in range(steps)], 0)
    lagk = []
    for lag in range(steps):
        lb = _cmul(col(pw[lag]), bbar)
        lagk.append(jnp.sum(c_re[:, :, :, None] * lb[0][:, None, :, :]
                            - c_im[:, :, :, None] * lb[1][:, None, :, :], axis=2))
    lagk = jnp.stack(lagk, 0)
    s_idx = jnp.arange(steps)[:, None]
    t_idx = jnp.arange(steps)[None, :]
    toe = jnp.where((s_idx <= t_idx)[:, :, None, None, None],
                    lagk[jnp.clip(t_idx - s_idx, 0, steps - 1)], 0.0)
    wc = []
    for t in range(steps):
        cl = _cmul((c_re, c_im), (pw[t + 1][0][:, None, :], pw[t + 1][1][:, None, :]))
        wc.append(jnp.stack([cl[0], -cl[1]], 0))
    wc = jnp.stack(wc, 0)

    s5a, s5c = [], []
    for hh in range(2):
        gs = slice(hh * gh, (hh + 1) * gh)
        wb_h = jnp.einsum("ab,srapc->sacrbp", eye, wb[:, :, gs]).reshape(steps * gh * chans, 2 * gh * states)
        toe_h = jnp.einsum("ab,stacd->sadtbc", eye, toe[:, :, gs]).reshape(steps * gh * chans, steps * gh * chans)
        wc_h = jnp.einsum("ab,tracp->raptbc", eye, wc[:, :, gs]).reshape(2 * gh * states, steps * gh * chans)
        s5a.append(jnp.concatenate([wb_h, toe_h], axis=1))
        s5c.append(wc_h)
    s5a = jnp.stack(s5a, 0).astype(bf16)
    s5c = jnp.stack(s5c, 0).astype(bf16)

    flat = lambda p: (p[0].reshape(1, groups * states), p[1].reshape(1, groups * states))
    m = flat(pw[steps])
    pk = [m]
    for _ in range(n.bit_length() - 2):
        pk.append(_cmul(pk[-1], pk[-1]))
    pj = m
    while pj[0].shape[0] < n:
        top = (pj[0][-1:], pj[1][-1:])
        nxt = _cmul(pj, top)
        pj = (jnp.concatenate([pj[0], nxt[0]], 0), jnp.concatenate([pj[1], nxt[1]], 0))
    pkr = jnp.concatenate([p[0] for p in pk], 0)
    pki = jnp.concatenate([p[1] for p in pk], 0)
    return s5a, s5c, pkr, pki, pj[0], pj[1]


def _layer_weights(l, p, n):
    row = lambda x: x.reshape(1, -1).astype(f32)
    heads = p["a_spatial_w"].shape[1]
    cid = jnp.arange(A_BLOCK) // CHUNK
    aw = jnp.where((cid[None, :] <= cid[:, None])[None], p["a_spatial_w"][l], 0.0).astype(bf16)
    ab = jnp.repeat(p["a_spatial_b"][l].T, HEAD_DIM, axis=1)
    eye_h = jnp.eye(heads, dtype=f32)
    wg = jnp.concatenate([jnp.einsum("ab,aij->aibj", eye_h, p["b_rg_w"][l]).reshape(heads * HEAD_DIM, -1),
                          jnp.einsum("ab,aij->aibj", eye_h, p["b_ig_w"][l]).reshape(heads * HEAD_DIM, -1)],
                         axis=1).astype(bf16)
    bg = jnp.concatenate([p["b_rg_b"][l], p["b_ig_b"][l]]).reshape(1, -1)
    nl = row(-LRU_C * jax.nn.softplus(-p["b_lambda"][l]))
    s5a, s5c, pkr, pki, pjr, pji = _s5_tables(
        p["c_a_re"][l], p["c_a_im"][l], p["c_log_dt"][l], p["c_b_re"][l], p["c_b_im"][l],
        p["c_c_re"][l], p["c_c_im"][l], n)
    d = p["w_in"].shape[1]
    wr = jnp.concatenate([p["router_group_w"][l].T,
                          jnp.transpose(p["router_expert_w"][l], (0, 2, 1)).reshape(-1, d)], 0)
    wr = jnp.concatenate([wr, jnp.zeros((ROUTER_ROWS - wr.shape[0], d), f32)], 0)
    wr_hi = wr.astype(bf16)
    wr_lo = (wr - wr_hi.astype(f32)).astype(bf16)
    rb = jnp.concatenate([p["router_group_b"][l], p["router_expert_b"][l].reshape(-1)])
    rb = jnp.concatenate([rb, jnp.zeros((ROUTER_ROWS - rb.shape[0],), f32)]).reshape(-1, 1)
    return [row(p["norm_mix_g"][l]), p["w_in"][l].astype(bf16), row(p["a_v_norm_g"][l]), aw, ab,
            p["b_conv_w"][l], row(p["b_conv_b"][l]), wg, bg, nl, s5a, s5c, pkr, pki, pjr, pji,
            row(p["c_d"][l]), p["c_glu_w"][l].astype(bf16), row(p["c_glu_b"][l]),
            row(p["mix_out_norm_g"][l]), p["w_out"][l].astype(bf16), row(p["norm_ffn_g"][l]),
            jnp.concatenate([wr_hi, wr_lo], 0), wr_hi, rb]


def _moe(xs, br8, cnt, w_gate, w_up, w_down, first_expert):
    t = br8.shape[1]
    counts = cnt[:N_BUCKETS, 0].astype(jnp.int32)
    padded = ((counts + MOE_BLOCK - 1) // MOE_BLOCK) * MOE_BLOCK
    pad_end = jnp.cumsum(padded)
    pad_start = pad_end - padded
    n_blocks = t // MOE_BLOCK + N_BUCKETS
    n_rows = n_blocks * MOE_BLOCK
    in_bucket = br8[0][:, None] == jnp.arange(N_BUCKETS, dtype=jnp.int32)[None, :]
    dest = br8[1] + jnp.sum(jnp.where(in_bucket, pad_start[None, :], 0), axis=1)
    tok = _invperm_call(pad_start + counts, pad_end, dest, n_rows, 2 * MOE_BLOCK)
    nv = (pad_end[-1] // MOE_BLOCK).astype(jnp.int32)
    blk_first = jnp.minimum(jnp.arange(n_blocks, dtype=jnp.int32), nv - 1) * MOE_BLOCK
    blk_bucket = jnp.minimum(
        jnp.sum((pad_end[None, :] <= blk_first[:, None]).astype(jnp.int32), axis=1), N_BUCKETS - 1)
    lo_tab = jnp.array([0, 0, 0, 1, 1, 2], jnp.int32)
    hi_tab = jnp.array([1, 2, 3, 2, 3, 3], jnp.int32)
    grp, pr = blk_bucket // N_PAIRS + first_expert // 4, blk_bucket % N_PAIRS
    ea = grp * 4 + jnp.sum(jnp.where(pr[:, None] == jnp.arange(N_PAIRS)[None, :], lo_tab[None, :], 0), axis=1)
    eb = grp * 4 + jnp.sum(jnp.where(pr[:, None] == jnp.arange(N_PAIRS)[None, :], hi_tab[None, :], 0), axis=1)
    return _expert_call(ea, eb, nv.reshape(1), tok.reshape(n_blocks, 1, MOE_BLOCK), xs,
                        w_gate, w_up, w_down, t)


def kernel(x, norm_mix_g, w_in, a_v_norm_g, a_spatial_w, a_spatial_b, b_conv_w, b_conv_b, b_rg_w, b_rg_b, b_ig_w, b_ig_b, b_lambda, c_a_re, c_a_im, c_log_dt, c_b_re, c_b_im, c_c_re, c_c_im, c_d, c_glu_w, c_glu_b, mix_out_norm_g, w_out, norm_ffn_g, router_group_w, router_group_b, router_expert_w, router_expert_b, expert_w_gate, expert_w_up, expert_w_down, final_norm_g):
    p = dict(locals())
    batch, seq, d = x.shape
    depth = w_in.shape[0]
    t = batch * seq
    assert d == SLAB_ROWS * LANES and seq % MIX_TILE == 0 and t % INV_TILE == 0
    n = MIX_TILE // TOKENS_PER_ROW
    idx = jnp.arange(MIX_TILE)
    rank_tri = (idx[:, None] < idx[None, :]).astype(bf16)
    n_exp = expert_w_gate.shape[1]
    wg_all, wu_all, wd_all = (w.astype(bf16).reshape((depth * n_exp,) + w.shape[2:])
                              for w in (expert_w_gate, expert_w_up, expert_w_down))
    h = x.reshape(t, d)
    res = None
    for l in range(depth):
        h, xs, br8, cnt = _mix_call(h, res, _layer_weights(l, p, n) + [rank_tri], batch, seq)
        res = _moe(xs, br8, cnt, wg_all, wu_all, wd_all, l * n_exp)
    return _final_call(h, res, final_norm_g.reshape(1, d)).reshape(batch, seq, d)
```
